```python
import math
import jax, jax.numpy as jnp
from jax import lax
import numpy as np

D_MODEL = 1024
BATCH = 8
SEQ = 2048
DEPTH = 2

POOL_WINDOWS = (2, 4, 8, 16)
POOL_WIDTH = D_MODEL // 2
POOL_GROUP = POOL_WIDTH // len(POOL_WINDOWS)
ATTN_HEADS = 16
ATTN_HEAD_DIM = D_MODEL // ATTN_HEADS
ATTN_WIDTH = ATTN_HEADS * ATTN_HEAD_DIM
IDX_HEADS = 8
IDX_HEAD_DIM = 64
TOPK_MAX = 256
TOPK_DIVISOR = 4
Q_BLOCK = 128
REL_BUCKETS = 32
REL_MAX_DISTANCE = 128
S5_WIDTH = D_MODEL // 2
S5_GROUP = 16
S5_GROUPS = S5_WIDTH // S5_GROUP
S5_STATE = 64
DT_MIN = 1e-3
DT_MAX = 1e-1
N_BRANCHES = 3
FFN_HIDDEN = ((8 * D_MODEL // 3 + 255) // 256) * 256
NORM_EPS = 1e-6

IN_SIZES = (POOL_WIDTH, ATTN_WIDTH, ATTN_HEAD_DIM, ATTN_HEAD_DIM, IDX_HEADS * IDX_HEAD_DIM, IDX_HEAD_DIM, IDX_HEADS, S5_WIDTH, N_BRANCHES * D_MODEL)
IN_WIDTH = POOL_WIDTH + ATTN_WIDTH + 2 * ATTN_HEAD_DIM + IDX_HEADS * IDX_HEAD_DIM + IDX_HEAD_DIM + IDX_HEADS + S5_WIDTH + N_BRANCHES * D_MODEL

kernel_name = 'hybrid_gated_pool_dsa_s5_block'


def rms_norm(x, g):
    xf = x.astype(jnp.float32)
    y = xf * lax.rsqrt(jnp.mean(xf * xf, axis=-1, keepdims=True) + NORM_EPS)
    return (y * g.astype(jnp.float32)).astype(x.dtype)


def split_columns(z):
    parts = []
    start = 0
    for size in IN_SIZES:
        parts.append(z[..., start:start + size])
        start += size
    return parts


def pool_mixer(u, mix_w, scale):
    b, s, _ = u.shape
    uf = u.astype(jnp.float32)
    csum = jnp.cumsum(uf, axis=1)
    pos = jnp.arange(s, dtype=jnp.float32)[:, None]
    outs = []
    for gi, w in enumerate(POOL_WINDOWS):
        sl = slice(gi * POOL_GROUP, (gi + 1) * POOL_GROUP)
        c = csum[..., sl]
        c_prev = jnp.pad(c, ((0, 0), (w, 0), (0, 0)))[:, :s]
        count = jnp.minimum(pos + 1.0, float(w))
        outs.append((c - c_prev) / count - uf[..., sl])
    d = jnp.stack(outs, axis=2).astype(u.dtype)
    y = jnp.einsum('bsgc,gcd->bsgd', d, mix_w).reshape(b, s, POOL_WIDTH)
    return y * scale


def rel_bucket(dist):
    max_exact = REL_BUCKETS // 2
    d_f = jnp.maximum(dist, 1).astype(jnp.float32)
    large = max_exact + (jnp.log(d_f / max_exact) / math.log(REL_MAX_DISTANCE / max_exact) * (REL_BUCKETS - max_exact)).astype(jnp.int32)
    large = jnp.minimum(large, REL_BUCKETS - 1)
    return jnp.where(dist < max_exact, dist, large)


def sparse_attention(q, k, v, qi, ki, wi, rel_bias):
    b, s = q.shape[0], q.shape[1]
    topk = min(TOPK_MAX, s // TOPK_DIVISOR)
    nb = s // Q_BLOCK
    idx_scale = (IDX_HEADS ** -0.5) * (IDX_HEAD_DIM ** -0.5)
    attn_scale = ATTN_HEAD_DIM ** -0.5
    key_pos = jnp.arange(s, dtype=jnp.int32)
    gather = jax.vmap(lambda table, idx: table[idx])

    def to_blocks(a):
        return jnp.moveaxis(a.reshape((b, nb, Q_BLOCK) + a.shape[2:]), 1, 0)

    def block(args):
        qb, qib, wib, start = args
        q_pos = start + jnp.arange(Q_BLOCK, dtype=jnp.int32)
        causal = key_pos[None, :] <= q_pos[:, None]
        rel = jax.nn.relu(jnp.einsum('bqhd,bsd->bqhs', qib, ki))
        score = jnp.einsum('bqhs,bqh->bqs', rel, wib) * idx_scale
        score = jnp.where(causal[None], score, -jnp.inf)
        _, sel = lax.top_k(score, topk)
        k_sel = gather(k, sel)
        v_sel = gather(v, sel)
        dist = q_pos[None, :, None] - sel
        bias = rel_bias[rel_bucket(jnp.maximum(dist, 0))]
        logits = jnp.einsum('bqhd,bqkd->bqhk', qb, k_sel).astype(jnp.float32) * attn_scale
        logits = logits + jnp.moveaxis(bias, -1, 2).astype(jnp.float32)
        logits = jnp.where((dist >= 0)[:, :, None, :], logits, jnp.finfo(jnp.float32).min)
        p = jax.nn.softmax(logits, axis=-1).astype(v.dtype)
        return jnp.einsum('bqhk,bqkd->bqhd', p, v_sel)

    starts = jnp.arange(nb, dtype=jnp.int32) * Q_BLOCK
    out = lax.map(block, (to_blocks(q), to_blocks(qi), to_blocks(wi), starts))
    return jnp.moveaxis(out, 0, 1).reshape(b, s, ATTN_WIDTH)


def _ssm_combine(earlier, later):
    a1r, a1i, b1r, b1i = earlier
    a2r, a2i, b2r, b2i = later
    ar = a2r * a1r - a2i * a1i
    ai = a2r * a1i + a2i * a1r
    br = a2r * b1r - a2i * b1i + b2r
    bi = a2r * b1i + a2i * b1r + b2i
    return (ar, ai, br, bi)


def s5_mixer(u, lam_re, lam_im, log_dt, b_re, b_im, c_re, c_im, d_skip, glu_w):
    b, s, _ = u.shape
    uf = u.astype(jnp.float32)
    ug = uf.reshape(b, s, S5_GROUPS, S5_GROUP)
    dt = jnp.exp(log_dt.astype(jnp.float32))[:, None]
    lr = lam_re.astype(jnp.float32)
    li = lam_im.astype(jnp.float32)
    mag = jnp.exp(lr * dt)
    a_re = mag * jnp.cos(li * dt)
    a_im = mag * jnp.sin(li * dt)
    den = lr * lr + li * li
    coef_re = ((a_re - 1.0) * lr + a_im * li) / den
    coef_im = (a_im * lr - (a_re - 1.0) * li) / den
    br = b_re.astype(jnp.float32)
    bi = b_im.astype(jnp.float32)
    bbar_re = coef_re[..., None] * br - coef_im[..., None] * bi
    bbar_im = coef_re[..., None] * bi + coef_im[..., None] * br
    bu_re = jnp.einsum('bsgi,gni->bsgn', ug, bbar_re)
    bu_im = jnp.einsum('bsgi,gni->bsgn', ug, bbar_im)
    a_re_t = jnp.broadcast_to(a_re[None, None], (1, s, S5_GROUPS, S5_STATE))
    a_im_t = jnp.broadcast_to(a_im[None, None], (1, s, S5_GROUPS, S5_STATE))
    _, _, x_re, x_im = lax.associative_scan(_ssm_combine, (a_re_t, a_im_t, bu_re, bu_im), axis=1)
    y = jnp.einsum('bsgn,gin->bsgi', x_re, c_re.astype(jnp.float32)) - jnp.einsum('bsgn,gin->bsgi', x_im, c_im.astype(jnp.float32))
    y = y.reshape(b, s, S5_WIDTH) + d_skip.astype(jnp.float32) * uf
    z = jax.nn.gelu(y).astype(u.dtype)
    a, g = jnp.split(z @ glu_w, 2, axis=-1)
    return a * jax.nn.sigmoid(g)


def setup_inputs(seed: int = 0) -> dict:
    key = jax.random.key(seed)
    ks = jax.random.split(key, 24)
    f32 = jnp.float32

    def nrm(k, shape, scale):
        return jax.random.normal(k, shape, f32) * scale

    def gain(k, width):
        return 1.0 + 0.02 * jax.random.normal(k, (DEPTH, width), f32)

    lam_im0 = jnp.pi * jnp.arange(S5_STATE, dtype=f32)
    return {
        'x': nrm(ks[0], (BATCH, SEQ, D_MODEL), 1.0),
        'norm_mix_pre': gain(ks[1], D_MODEL),
        'norm_mix_post': gain(ks[2], D_MODEL),
        'norm_ffn_pre': gain(ks[3], D_MODEL),
        'norm_ffn_post': gain(ks[4], D_MODEL),
        'w_in': nrm(ks[5], (DEPTH, D_MODEL, IN_WIDTH), D_MODEL ** -0.5),
        'pool_mix_w': nrm(ks[6], (DEPTH, len(POOL_WINDOWS), POOL_GROUP, POOL_GROUP), POOL_GROUP ** -0.5),
        'pool_scale': gain(ks[7], POOL_WIDTH),
        'pool_out_w': nrm(ks[8], (DEPTH, POOL_WIDTH, D_MODEL), POOL_WIDTH ** -0.5),
        'attn_out_w': nrm(ks[9], (DEPTH, ATTN_WIDTH, D_MODEL), ATTN_WIDTH ** -0.5),
        'rel_bias': nrm(ks[10], (REL_BUCKETS, ATTN_HEADS), 0.5),
        's5_lambda_re': -0.5 + 0.01 * jax.random.normal(ks[11], (DEPTH, S5_GROUPS, S5_STATE), f32),
        's5_lambda_im': lam_im0[None, None, :] + 0.01 * jax.random.normal(ks[12], (DEPTH, S5_GROUPS, S5_STATE), f32),
        's5_log_dt': jax.random.uniform(ks[13], (DEPTH, S5_GROUPS), f32, math.log(DT_MIN), math.log(DT_MAX)),
        's5_b_re': nrm(ks[14], (DEPTH, S5_GROUPS, S5_STATE, S5_GROUP), (2 * S5_GROUP) ** -0.5),
        's5_b_im': nrm(ks[15], (DEPTH, S5_GROUPS, S5_STATE, S5_GROUP), (2 * S5_GROUP) ** -0.5),
        's5_c_re': nrm(ks[16], (DEPTH, S5_GROUPS, S5_GROUP, S5_STATE), S5_STATE ** -0.5),
        's5_c_im': nrm(ks[17], (DEPTH, S5_GROUPS, S5_GROUP, S5_STATE), S5_STATE ** -0.5),
        's5_d': nrm(ks[18], (DEPTH, S5_WIDTH), 1.0),
        's5_glu_w': nrm(ks[19], (DEPTH, S5_WIDTH, 2 * D_MODEL), S5_WIDTH ** -0.5),
        'w_out': nrm(ks[20], (DEPTH, D_MODEL, D_MODEL), D_MODEL ** -0.5),
        'ffn_w_in': nrm(ks[21], (DEPTH, D_MODEL, 2 * FFN_HIDDEN), D_MODEL ** -0.5),
        'ffn_w_out': nrm(ks[22], (DEPTH, FFN_HIDDEN, D_MODEL), FFN_HIDDEN ** -0.5),
    }


def reference(x, norm_mix_pre, norm_mix_post, norm_ffn_pre, norm_ffn_post, w_in, pool_mix_w, pool_scale, pool_out_w, attn_out_w, rel_bias, s5_lambda_re, s5_lambda_im, s5_log_dt, s5_b_re, s5_b_im, s5_c_re, s5_c_im, s5_d, s5_glu_w, w_out, ffn_w_in, ffn_w_out):
    b, s, _ = x.shape
    for l in range(DEPTH):
        h = rms_norm(x, norm_mix_pre[l])
        z = h @ w_in[l]
        pool_u, q, k, v, qi, ki, wi, s5_u, gates = split_columns(z)
        y_pool = pool_mixer(pool_u, pool_mix_w[l], pool_scale[l]) @ pool_out_w[l]
        y_attn = sparse_attention(q.reshape(b, s, ATTN_HEADS, ATTN_HEAD_DIM), k, v, qi.reshape(b, s, IDX_HEADS, IDX_HEAD_DIM), ki, wi, rel_bias) @ attn_out_w[l]
        y_s5 = s5_mixer(s5_u, s5_lambda_re[l], s5_lambda_im[l], s5_log_dt[l], s5_b_re[l], s5_b_im[l], s5_c_re[l], s5_c_im[l], s5_d[l], s5_glu_w[l])
        g = jax.nn.sigmoid(gates.reshape(b, s, N_BRANCHES, D_MODEL))
        merged = g[:, :, 0] * y_pool + g[:, :, 1] * y_attn + g[:, :, 2] * y_s5
        x = x + rms_norm(merged @ w_out[l], norm_mix_post[l])
        h = rms_norm(x, norm_ffn_pre[l])
        gate, up = jnp.split(h @ ffn_w_in[l], 2, axis=-1)
        f = (jax.nn.silu(gate) * up) @ ffn_w_out[l]
        x = x + rms_norm(f, norm_ffn_post[l])
    return x
```

```python
import functools
import math

import jax
import jax.numpy as jnp
from jax import lax
from jax.experimental import pallas as pl
from jax.experimental.pallas import tpu as pltpu

F32 = jnp.float32
BF16 = jnp.bfloat16
I32 = jnp.int32

D_MODEL = 1024
POOL_WINDOWS = (2, 4, 8, 16)
POOL_WIDTH = 512
POOL_GROUP = 128
POOL_HALO = 16
ATTN_HEADS = 16
ATTN_HEAD_DIM = 64
ATTN_WIDTH = 1024
IDX_HEADS = 8
IDX_HEAD_DIM = 64
TOPK_MAX = 256
TOPK_DIVISOR = 4
Q_BLOCK = 128
REL_BUCKETS = 32
REL_MAX_DISTANCE = 128
S5_WIDTH = 512
S5_GROUP = 16
S5_GROUPS = 32
S5_STATE = 64
S5_LANES = S5_GROUPS * S5_STATE
N_BRANCHES = 3
FFN_HIDDEN = 2816
NORM_EPS = 1e-6

COL_POOL = 0
COL_S5 = 512
COL_Q = 1024
COL_K = 2048
COL_V = 2176
COL_KI = 2304
ROW_WIDTH = 2560
IDX_T_ROWS = 528

NEG_BIG = -1e30
INT_MIN = -(2 ** 31)

VMEM_LIMIT = 56 * 1024 * 1024


def _rms(x, g):
    ms = jnp.mean(x * x, axis=-1, keepdims=True)
    return x * lax.rsqrt(ms + NORM_EPS) * g


def _dot(a, b):
    return jnp.dot(a, b, preferred_element_type=F32)


def _dot_nt(a, b):
    return lax.dot_general(a, b, (((1,), (1,)), ((), ())), preferred_element_type=F32)


def _params(*sem):
    return pltpu.CompilerParams(dimension_semantics=sem, vmem_limit_bytes=VMEM_LIMIT)


def _in_proj_rows_kernel(x_ref, g_ref, w_ref, o_ref, h_ref):
    @pl.when(pl.program_id(1) == 0)
    def _():
        h_ref[...] = _rms(x_ref[...], g_ref[...]).astype(BF16)

    o_ref[...] = _dot(h_ref[...], w_ref[...])


def in_proj_rows(x, gain, w, *, tm=512, tn=512):
    t, d = x.shape
    n = w.shape[1]
    return pl.pallas_call(
        _in_proj_rows_kernel,
        out_shape=jax.ShapeDtypeStruct((t, n), F32),
        grid=(t // tm, n // tn),
        in_specs=[
            pl.BlockSpec((tm, d), lambda i, j: (i, 0)),
            pl.BlockSpec((1, d), lambda i, j: (0, 0)),
            pl.BlockSpec((d, tn), lambda i, j: (0, j)),
        ],
        out_specs=pl.BlockSpec((tm, tn), lambda i, j: (i, j)),
        scratch_shapes=[pltpu.VMEM((tm, d), BF16)],
        compiler_params=_params("parallel", "arbitrary"),
        name="in_proj_rows",
    )(x, gain, w)


def _in_proj_cols_kernel(x_ref, g_ref, wt_ref, o_ref):
    h = _rms(x_ref[...], g_ref[...]).astype(BF16)
    o_ref[...] = _dot_nt(wt_ref[...], h)


def in_proj_cols(x, gain, wt, *, tm=512):
    t, d = x.shape
    c = wt.shape[0]
    return pl.pallas_call(
        _in_proj_cols_kernel,
        out_shape=jax.ShapeDtypeStruct((c, t), F32),
        grid=(t // tm,),
        in_specs=[
            pl.BlockSpec((tm, d), lambda i: (i, 0)),
            pl.BlockSpec((1, d), lambda i: (0, 0)),
            pl.BlockSpec((c, d), lambda i: (0, 0)),
        ],
        out_specs=pl.BlockSpec((c, tm), lambda i: (0, i)),
        compiler_params=_params("parallel"),
        name="in_proj_cols",
    )(x, gain, wt)


def _pool_kernel(u_ref, mixw_ref, scale_ref, o_ref, buf_ref):
    s = u_ref.shape[0]
    buf_ref[0:POOL_HALO, :] = jnp.zeros((POOL_HALO, POOL_WIDTH), F32)
    buf_ref[POOL_HALO:POOL_HALO + s, :] = u_ref[...]
    pos = lax.broadcasted_iota(I32, (s, POOL_GROUP), 0).astype(F32)
    for gi, w in enumerate(POOL_WINDOWS):
        c0 = gi * POOL_GROUP
        tok = buf_ref[POOL_HALO:POOL_HALO + s, c0:c0 + POOL_GROUP]
        acc = tok
        for k in range(1, w):
            acc = acc + buf_ref[POOL_HALO - k:POOL_HALO - k + s, c0:c0 + POOL_GROUP]
        count = jnp.minimum(pos + 1.0, float(w))
        d = acc / count - tok
        y = _dot(d.astype(BF16), mixw_ref[gi])
        o_ref[:, c0:c0 + POOL_GROUP] = (y * scale_ref[:, c0:c0 + POOL_GROUP]).astype(BF16)


def pool_mixer(zrow, mix_w, scale):
    b, s, _ = zrow.shape
    return pl.pallas_call(
        _pool_kernel,
        out_shape=jax.ShapeDtypeStruct((b, s, POOL_WIDTH), BF16),
        grid=(b,),
        in_specs=[
            pl.BlockSpec((None, s, POOL_WIDTH), lambda i: (i, 0, COL_POOL // POOL_WIDTH)),
            pl.BlockSpec((len(POOL_WINDOWS), POOL_GROUP, POOL_GROUP), lambda i: (0, 0, 0)),
            pl.BlockSpec((1, POOL_WIDTH), lambda i: (0, 0)),
        ],
        out_specs=pl.BlockSpec((None, s, POOL_WIDTH), lambda i: (i, 0, 0)),
        scratch_shapes=[pltpu.VMEM((POOL_HALO + s, POOL_WIDTH), F32)],
        compiler_params=_params("parallel"),
        name="pool_mixer",
    )(zrow, mix_w, scale)


def _s5_disc_kernel(lr_ref, li_ref, ldt_ref, br_ref, bi_ref, ar_ref, ai_ref, bbr_ref, bbi_ref):
    lr = lr_ref[...]
    li = li_ref[...]
    dt = jnp.exp(ldt_ref[...])
    mag = jnp.exp(lr * dt)
    a_re = mag * jnp.cos(li * dt)
    a_im = mag * jnp.sin(li * dt)
    den = lr * lr + li * li
    coef_re = ((a_re - 1.0) * lr + a_im * li) / den
    coef_im = (a_im * lr - (a_re - 1.0) * li) / den
    ar_ref[...] = a_re
    ai_ref[...] = a_im
    br = br_ref[...]
    bi = bi_ref[...]
    bbr_ref[...] = coef_re * br - coef_im * bi
    bbi_ref[...] = coef_re * bi + coef_im * br


def s5_discretise(lam_re, lam_im, log_dt, b_re_t, b_im_t):
    g, _, n = lam_re.shape
    j = b_re_t.shape[1]
    return pl.pallas_call(
        _s5_disc_kernel,
        out_shape=(
            jax.ShapeDtypeStruct((g, 1, n), F32),
            jax.ShapeDtypeStruct((g, 1, n), F32),
            jax.ShapeDtypeStruct((g, j, n), F32),
            jax.ShapeDtypeStruct((g, j, n), F32),
        ),
        name="s5_discretise",
    )(lam_re, lam_im, log_dt, b_re_t, b_im_t)


def _s5_scan_kernel(u_ref, bre_ref, bim_ref, cre_ref, cim_ref, ar_ref, ai_ref, d_ref, o_ref,
                    ut_ref, xr_ref, xi_ref, sr_ref, si_ref, y_ref, *, lane_chunk):
    nb, L, _ = u_ref.shape

    @pl.when(pl.program_id(0) == 0)
    def _():
        sr_ref[...] = jnp.zeros_like(sr_ref)
        si_ref[...] = jnp.zeros_like(si_ref)

    n_slab = S5_WIDTH // 128
    for b in range(nb):
        for c in range(n_slab):
            ut_ref[c, pl.ds(b, L, stride=nb), :] = u_ref[b, :, c * 128:(c + 1) * 128]
    u_t = jnp.concatenate([ut_ref[c] for c in range(n_slab)], axis=1)
    ub = u_t.astype(BF16)
    xr_ref[...] = _dot(ub, bre_ref[...])
    xi_ref[...] = _dot(ub, bim_ref[...])

    for c in range(S5_LANES // lane_chunk):
        cs = slice(c * lane_chunk, (c + 1) * lane_chunk)
        a_re = jnp.broadcast_to(ar_ref[:, cs], (nb, lane_chunk))
        a_im = jnp.broadcast_to(ai_ref[:, cs], (nb, lane_chunk))

        def step(t, carry, cs=cs, a_re=a_re, a_im=a_im):
            s_re, s_im = carry
            r0 = pl.multiple_of(t * nb, nb)
            n_re = a_re * s_re - a_im * s_im + xr_ref[pl.ds(r0, nb), cs]
            n_im = a_re * s_im + a_im * s_re + xi_ref[pl.ds(r0, nb), cs]
            xr_ref[pl.ds(r0, nb), cs] = n_re
            xi_ref[pl.ds(r0, nb), cs] = n_im
            return n_re, n_im

        s_re, s_im = lax.fori_loop(0, L, step, (sr_ref[:, cs], si_ref[:, cs]), unroll=2)
        sr_ref[:, cs] = s_re
        si_ref[:, cs] = s_im

    y = _dot(xr_ref[...].astype(BF16), cre_ref[...]) - _dot(xi_ref[...].astype(BF16), cim_ref[...])
    y = jax.nn.gelu(y + d_ref[...] * u_t)
    for c in range(n_slab):
        y_ref[c] = y[:, c * 128:(c + 1) * 128]
    for b in range(nb):
        for c in range(n_slab):
            o_ref[b, :, c * 128:(c + 1) * 128] = y_ref[c, pl.ds(b, L, stride=nb), :].astype(BF16)


def s5_scan(zrow, b_re_full, b_im_full, c_re_full, c_im_full, a_re, a_im, d_skip, *, chunk=64, lane_chunk=1024):
    b, s, _ = zrow.shape
    rows = chunk * b
    const = lambda i: (0, 0)
    return pl.pallas_call(
        functools.partial(_s5_scan_kernel, lane_chunk=lane_chunk),
        out_shape=jax.ShapeDtypeStruct((b, s, S5_WIDTH), BF16),
        grid=(s // chunk,),
        in_specs=[
            pl.BlockSpec((b, chunk, S5_WIDTH), lambda i: (0, i, COL_S5 // S5_WIDTH)),
            pl.BlockSpec((S5_WIDTH, S5_LANES), const),
            pl.BlockSpec((S5_WIDTH, S5_LANES), const),
            pl.BlockSpec((S5_LANES, S5_WIDTH), const),
            pl.BlockSpec((S5_LANES, S5_WIDTH), const),
            pl.BlockSpec((1, S5_LANES), const),
            pl.BlockSpec((1, S5_LANES), const),
            pl.BlockSpec((1, S5_WIDTH), const),
        ],
        out_specs=pl.BlockSpec((b, chunk, S5_WIDTH), lambda i: (0, i, 0)),
        scratch_shapes=[
            pltpu.VMEM((S5_WIDTH // 128, rows, 128), F32),
            pltpu.VMEM((rows, S5_LANES), F32),
            pltpu.VMEM((rows, S5_LANES), F32),
            pltpu.VMEM((b, S5_LANES), F32),
            pltpu.VMEM((b, S5_LANES), F32),
            pltpu.VMEM((S5_WIDTH // 128, rows, 128), F32),
        ],
        compiler_params=_params("arbitrary"),
        name="s5_scan",
    )(zrow, b_re_full, b_im_full, c_re_full, c_im_full, a_re, a_im, d_skip)


def _rel_bias_kernel(rb_ref, o_ref):
    p = pl.program_id(0)
    h = pl.program_id(1)
    ql = lax.broadcasted_iota(I32, (Q_BLOCK, 2 * Q_BLOCK), 0)
    c = lax.broadcasted_iota(I32, (Q_BLOCK, 2 * Q_BLOCK), 1)
    dist = jnp.maximum(ql - c + p * Q_BLOCK, 0)
    max_exact = REL_BUCKETS // 2
    d_f = jnp.maximum(dist, 1).astype(F32)
    large = max_exact + (jnp.log(d_f / max_exact) / math.log(REL_MAX_DISTANCE / max_exact)
                         * (REL_BUCKETS - max_exact)).astype(I32)
    large = jnp.minimum(large, REL_BUCKETS - 1)
    bucket = jnp.where(dist < max_exact, dist, large)
    acc = jnp.zeros((Q_BLOCK, 2 * Q_BLOCK), F32)
    for k in range(REL_BUCKETS):
        acc = jnp.where(bucket == k, rb_ref[k, h], acc)
    o_ref[...] = acc


def rel_bias_tiles(rel_bias):
    return pl.pallas_call(
        _rel_bias_kernel,
        out_shape=jax.ShapeDtypeStruct((2, ATTN_HEADS, Q_BLOCK, 2 * Q_BLOCK), F32),
        grid=(2, ATTN_HEADS),
        in_specs=[pl.BlockSpec(memory_space=pltpu.SMEM)],
        out_specs=pl.BlockSpec((None, None, Q_BLOCK, 2 * Q_BLOCK), lambda p, h: (p, h, 0, 0)),
        name="rel_bias_tiles",
    )(rel_bias)


def _count(mask):
    return jnp.sum(mask.astype(I32), axis=0, keepdims=True)


def _dsa_kernel(rb_ref, q_ref, k_ref, v_ref, ki_ref, qit_ref, wit_ref, bias_ref, o_ref,
                kb_ref, vb_ref, kib_ref, q2_ref, key_ref, tie_ref, mbt_ref, mbfar_ref, oh_ref, *, qb0, topk):
    sc = k_ref.shape[0]
    nt = sc // Q_BLOCK
    i = pl.program_id(1) + qb0
    n0 = jnp.maximum(i - 1, 0)

    @pl.when(pl.program_id(1) == 0)
    def _():
        kb_ref[...] = k_ref[:, 0:ATTN_HEAD_DIM].astype(BF16)
        vb_ref[...] = v_ref[:, 0:ATTN_HEAD_DIM].astype(BF16)
        kib_ref[...] = ki_ref[:, 0:IDX_HEAD_DIM].astype(BF16)

    kib = kib_ref[...]
    score = jnp.zeros((sc, Q_BLOCK), F32)
    for h in range(IDX_HEADS):
        qh = qit_ref[h * IDX_HEAD_DIM:(h + 1) * IDX_HEAD_DIM, :].astype(BF16)
        rel = jnp.maximum(_dot(kib, qh), 0.0)
        score = score + rel * wit_ref[h:h + 1, :]
    score = score * ((IDX_HEADS ** -0.5) * (IDX_HEAD_DIM ** -0.5))
    kpos = lax.broadcasted_iota(I32, (sc, Q_BLOCK), 0)
    qpos = lax.broadcasted_iota(I32, (sc, Q_BLOCK), 1) + i * Q_BLOCK
    causal = kpos <= qpos
    score = jnp.where(causal, score + 0.0, -jnp.inf)
    bits = pltpu.bitcast(score, I32)
    key_ref[...] = jnp.where(bits < 0, bits ^ 0x7FFFFFFF, bits)

    def value_step(t, thr):
        cand = thr + jnp.left_shift(jnp.int32(1), 31 - t)
        cnt = _count(key_ref[...] >= cand)
        return jnp.where(cnt >= topk, cand, thr)

    thr = lax.fori_loop(0, 32, value_step, jnp.full((1, Q_BLOCK), INT_MIN, I32))

    need = topk - _count(key_ref[...] > thr)
    tie_ref[...] = jnp.where(key_ref[...] == thr, kpos, jnp.int32(2 ** 30))

    def index_step(t, m):
        cand = m + jnp.left_shift(jnp.int32(1), 10 - t)
        cnt = _count(tie_ref[...] < cand)
        return jnp.where(cnt < need, cand, m)

    m = lax.fori_loop(0, 11, index_step, jnp.zeros((1, Q_BLOCK), I32))
    causal_bias = jnp.where(kpos <= qpos, 0.0, NEG_BIG)
    tie_bias = jnp.where(tie_ref[...] <= m, causal_bias, NEG_BIG)
    mbt = jnp.where(key_ref[...] > thr, causal_bias, tie_bias)

    for j in range(nt):
        tile = mbt[j * Q_BLOCK:(j + 1) * Q_BLOCK, :].T
        mbt_ref[j] = tile
        mbfar_ref[:, j * Q_BLOCK:(j + 1) * Q_BLOCK] = jnp.where(j < n0, tile, NEG_BIG)
    mb_near = jnp.concatenate([mbt_ref[n0], mbt_ref[n0 + 1]], axis=1)

    q = q_ref[...] * (ATTN_HEAD_DIM ** -0.5)
    for h in range(ATTN_HEADS):
        q2_ref[h] = q[:, h * ATTN_HEAD_DIM:(h + 1) * ATTN_HEAD_DIM].astype(BF16)

    near0 = pl.multiple_of(n0 * Q_BLOCK, Q_BLOCK)
    k_near = kb_ref[pl.ds(near0, 2 * Q_BLOCK), :]
    v_near = vb_ref[pl.ds(near0, 2 * Q_BLOCK), :]
    pidx = jnp.minimum(i, 1)

    def head(h, carry):
        qh = q2_ref[h]
        lf = _dot_nt(qh, kb_ref[...]) + rb_ref[REL_BUCKETS - 1, h] + mbfar_ref[...]
        ln = _dot_nt(qh, k_near) + bias_ref[pidx, h] + mb_near
        mx = jnp.maximum(jnp.max(lf, axis=1, keepdims=True), jnp.max(ln, axis=1, keepdims=True))
        pf = jnp.exp(lf - mx)
        pn = jnp.exp(ln - mx)
        den = jnp.sum(pf, axis=1, keepdims=True) + jnp.sum(pn, axis=1, keepdims=True)
        acc = _dot(pf.astype(BF16), vb_ref[...]) + _dot(pn.astype(BF16), v_near)
        oh_ref[h] = acc / den
        return carry

    lax.fori_loop(0, ATTN_HEADS, head, 0)
    for h in range(ATTN_HEADS):
        o_ref[:, h * ATTN_HEAD_DIM:(h + 1) * ATTN_HEAD_DIM] = oh_ref[h]


def dsa_attention(zrow, idx_t, bias_tiles, rel_bias, *, qb0, qb1):
    b, s, _ = zrow.shape
    sc = qb1 * Q_BLOCK
    nq = qb1 - qb0
    nqs = s // Q_BLOCK
    topk = min(TOPK_MAX, s // TOPK_DIVISOR)
    kv_spec = lambda col: pl.BlockSpec((None, sc, 128), lambda bi, qi: (bi, 0, col // 128))
    return pl.pallas_call(
        functools.partial(_dsa_kernel, qb0=qb0, topk=topk),
        out_shape=jax.ShapeDtypeStruct((b, nq * Q_BLOCK, ATTN_WIDTH), F32),
        grid=(b, nq),
        in_specs=[
            pl.BlockSpec(memory_space=pltpu.SMEM),
            pl.BlockSpec((None, Q_BLOCK, ATTN_WIDTH), lambda bi, qi: (bi, qi + qb0, COL_Q // ATTN_WIDTH)),
            kv_spec(COL_K),
            kv_spec(COL_V),
            kv_spec(COL_KI),
            pl.BlockSpec((IDX_HEADS * IDX_HEAD_DIM, Q_BLOCK), lambda bi, qi: (0, bi * nqs + qi + qb0)),
            pl.BlockSpec((16, Q_BLOCK), lambda bi, qi: (IDX_HEADS * IDX_HEAD_DIM // 16, bi * nqs + qi + qb0)),
            pl.BlockSpec((2, ATTN_HEADS, Q_BLOCK, 2 * Q_BLOCK), lambda bi, qi: (0, 0, 0, 0)),
        ],
        out_specs=pl.BlockSpec((None, Q_BLOCK, ATTN_WIDTH), lambda bi, qi: (bi, qi, 0)),
        scratch_shapes=[
            pltpu.VMEM((sc, ATTN_HEAD_DIM), BF16),
            pltpu.VMEM((sc, ATTN_HEAD_DIM), BF16),
            pltpu.VMEM((sc, IDX_HEAD_DIM), BF16),
            pltpu.VMEM((ATTN_HEADS, Q_BLOCK, ATTN_HEAD_DIM), BF16),
            pltpu.VMEM((sc, Q_BLOCK), I32),
            pltpu.VMEM((sc, Q_BLOCK), I32),
            pltpu.VMEM((sc // Q_BLOCK, Q_BLOCK, Q_BLOCK), F32),
            pltpu.VMEM((Q_BLOCK, sc), F32),
            pltpu.VMEM((ATTN_HEADS, Q_BLOCK, ATTN_HEAD_DIM), F32),
        ],
        compiler_params=_params("parallel", "arbitrary"),
        name="dsa_attention",
    )(rel_bias, zrow, zrow, zrow, zrow, idx_t, idx_t, bias_tiles)


def _merge_kernel(x_ref, gpre_ref, wg_ref, pool_ref, wpool_ref, attn_ref, wattn_ref, s5_ref, wglu_ref,
                  wout_ref, gpost_ref, o_ref):
    x = x_ref[...]
    h = _rms(x, gpre_ref[...]).astype(BF16)
    gates = jax.nn.sigmoid(_dot(h, wg_ref[...]))
    y_pool = _dot(pool_ref[...], wpool_ref[...])
    y_attn = _dot(attn_ref[...].astype(BF16), wattn_ref[...])
    glu = _dot(s5_ref[...], wglu_ref[...])
    y_s5 = glu[:, :D_MODEL] * jax.nn.sigmoid(glu[:, D_MODEL:])
    merged = (gates[:, 0:D_MODEL] * y_pool + gates[:, D_MODEL:2 * D_MODEL] * y_attn
              + gates[:, 2 * D_MODEL:3 * D_MODEL] * y_s5)
    mixed = _dot(merged.astype(BF16), wout_ref[...])
    o_ref[...] = x + _rms(mixed, gpost_ref[...])


def merge(x, g_pre, w_gates, pool_y, w_pool, attn_y, w_attn, s5_y, w_glu, w_out, g_post, *, tm=256):
    t, d = x.shape
    row = lambda width: pl.BlockSpec((tm, width), lambda i: (i, 0))
    full = lambda a: pl.BlockSpec(a.shape, lambda i: (0, 0))
    return pl.pallas_call(
        _merge_kernel,
        out_shape=jax.ShapeDtypeStruct((t, d), F32),
        grid=(t // tm,),
        in_specs=[row(d), full(g_pre), full(w_gates), row(POOL_WIDTH), full(w_pool), row(ATTN_WIDTH),
                  full(w_attn), row(S5_WIDTH), full(w_glu), full(w_out), full(g_post)],
        out_specs=row(d),
        compiler_params=_params("parallel"),
        name="merge",
    )(x, g_pre, w_gates, pool_y, w_pool, attn_y, w_attn, s5_y, w_glu, w_out, g_post)


def _ffn_kernel(x_ref, gpre_ref, wgate_ref, wup_ref, wdown_ref, gpost_ref, o_ref, h_ref, acc_ref):
    j = pl.program_id(1)

    @pl.when(j == 0)
    def _():
        h_ref[...] = _rms(x_ref[...], gpre_ref[...]).astype(BF16)
        acc_ref[...] = jnp.zeros_like(acc_ref)

    h = h_ref[...]
    gate = _dot(h, wgate_ref[...])
    up = _dot(h, wup_ref[...])
    act = (jax.nn.silu(gate) * up).astype(BF16)
    acc_ref[...] += _dot(act, wdown_ref[...])

    @pl.when(j == pl.num_programs(1) - 1)
    def _():
        o_ref[...] = x_ref[...] + _rms(acc_ref[...], gpost_ref[...])


def ffn(x, g_pre, w_in, w_down, g_post, *, tm=512, th=256):
    t, d = x.shape
    hid = w_down.shape[0]
    nh = hid // th
    return pl.pallas_call(
        _ffn_kernel,
        out_shape=jax.ShapeDtypeStruct((t, d), F32),
        grid=(t // tm, nh),
        in_specs=[
            pl.BlockSpec((tm, d), lambda i, j: (i, 0)),
            pl.BlockSpec((1, d), lambda i, j: (0, 0)),
            pl.BlockSpec((d, th), lambda i, j: (0, j)),
            pl.BlockSpec((d, th), lambda i, j: (0, j + nh)),
            pl.BlockSpec((th, d), lambda i, j: (j, 0)),
            pl.BlockSpec((1, d), lambda i, j: (0, 0)),
        ],
        out_specs=pl.BlockSpec((tm, d), lambda i, j: (i, 0)),
        scratch_shapes=[pltpu.VMEM((tm, d), BF16), pltpu.VMEM((tm, d), F32)],
        compiler_params=_params("parallel", "arbitrary"),
        name="ffn",
    )(x, g_pre, w_in, w_in, w_down, g_post)


def _pad_cols(w, width):
    return jnp.pad(w, ((0, 0), (0, width - w.shape[1])))


def _split_w_in(w):
    sizes = (POOL_WIDTH, ATTN_WIDTH, ATTN_HEAD_DIM, ATTN_HEAD_DIM, IDX_HEADS * IDX_HEAD_DIM, IDX_HEAD_DIM,
             IDX_HEADS, S5_WIDTH, N_BRANCHES * D_MODEL)
    parts, start = [], 0
    for size in sizes:
        parts.append(w[:, start:start + size])
        start += size
    w_pool, w_q, w_k, w_v, w_qi, w_ki, w_wi, w_s5, w_gates = parts
    w_rows = jnp.concatenate(
        [w_pool, w_s5, w_q, _pad_cols(w_k, 128), _pad_cols(w_v, 128), _pad_cols(w_ki, ROW_WIDTH - COL_KI)], axis=1)
    w_cols = jnp.concatenate([w_qi, _pad_cols(w_wi, IDX_T_ROWS - IDX_HEADS * IDX_HEAD_DIM)], axis=1).T
    return w_rows.astype(BF16), w_cols.astype(BF16), w_gates.astype(BF16)


def _block_diag(blocks):
    g, r, c = blocks.shape
    eye = jnp.eye(g, dtype=blocks.dtype)
    return (blocks[:, :, None, :] * eye[:, None, :, None]).reshape(g * r, g * c)


def _attn_classes(nqs):
    n_classes = 4 if nqs % 4 == 0 and nqs >= 8 else 1
    step = nqs // n_classes
    return [(c * step, (c + 1) * step) for c in range(n_classes)]


def kernel(x, norm_mix_pre, norm_mix_post, norm_ffn_pre, norm_ffn_post, w_in, pool_mix_w, pool_scale, pool_out_w, attn_out_w, rel_bias, s5_lambda_re, s5_lambda_im, s5_log_dt, s5_b_re, s5_b_im, s5_c_re, s5_c_im, s5_d, s5_glu_w, w_out, ffn_w_in, ffn_w_out):
    b, s, d = x.shape
    t = b * s
    depth = w_in.shape[0]
    bias_tiles = rel_bias_tiles(rel_bias)
    xf = x.reshape(t, d)
    for l in range(depth):
        w_rows, w_cols, w_gates = _split_w_in(w_in[l])
        g_pre = norm_mix_pre[l][None, :]
        zrow = in_proj_rows(xf, g_pre, w_rows).reshape(b, s, ROW_WIDTH)
        idx_t = in_proj_cols(xf, g_pre, w_cols)

        pool_y = pool_mixer(zrow, pool_mix_w[l].astype(BF16), pool_scale[l][None, :])

        a_re, a_im, bb_re, bb_im = s5_discretise(
            s5_lambda_re[l][:, None, :], s5_lambda_im[l][:, None, :], s5_log_dt[l][:, None, None],
            jnp.swapaxes(s5_b_re[l], 1, 2), jnp.swapaxes(s5_b_im[l], 1, 2))
        s5_y = s5_scan(
            zrow,
            _block_diag(bb_re).astype(BF16), _block_diag(bb_im).astype(BF16),
            _block_diag(jnp.swapaxes(s5_c_re[l], 1, 2)).astype(BF16),
            _block_diag(jnp.swapaxes(s5_c_im[l], 1, 2)).astype(BF16),
            a_re.reshape(1, S5_LANES), a_im.reshape(1, S5_LANES), s5_d[l][None, :])

        attn_y = jnp.concatenate(
            [dsa_attention(zrow, idx_t, bias_tiles, rel_bias, qb0=q0, qb1=q1)
             for q0, q1 in _attn_classes(s // Q_BLOCK)], axis=1)

        xf = merge(xf, g_pre, w_gates, pool_y.reshape(t, POOL_WIDTH), pool_out_w[l].astype(BF16),
                   attn_y.reshape(t, ATTN_WIDTH), attn_out_w[l].astype(BF16),
                   s5_y.reshape(t, S5_WIDTH), s5_glu_w[l].astype(BF16), w_out[l].astype(BF16),
                   norm_mix_post[l][None, :])
        xf = ffn(xf, norm_ffn_pre[l][None, :], ffn_w_in[l].astype(BF16), ffn_w_out[l].astype(BF16),
                 norm_ffn_post[l][None, :])
    return xf.reshape(b, s, d)
```

```python
import functools
import math

import jax
import jax.numpy as jnp
from jax import lax
from jax.experimental import pallas as pl
from jax.experimental.pallas import tpu as pltpu

F32 = jnp.float32
BF16 = jnp.bfloat16
I32 = jnp.int32

D_MODEL = 1024
POOL_WINDOWS = (2, 4, 8, 16)
POOL_WIDTH = 512
POOL_GROUP = 128
POOL_HALO = 16
ATTN_HEADS = 16
ATTN_HEAD_DIM = 64
ATTN_WIDTH = 1024
IDX_HEADS = 8
IDX_HEAD_DIM = 64
TOPK_MAX = 256
TOPK_DIVISOR = 4
Q_BLOCK = 128
REL_BUCKETS = 32
REL_MAX_DISTANCE = 128
S5_WIDTH = 512
S5_GROUP = 16
S5_GROUPS = 32
S5_STATE = 64
S5_LANES = S5_GROUPS * S5_STATE
N_BRANCHES = 3
FFN_HIDDEN = 2816
NORM_EPS = 1e-6

COL_POOL = 0
COL_S5 = 512
COL_Q = 1024
COL_K = 2048
COL_V = 2176
COL_KI = 2304
ROW_WIDTH = 2560
IDX_T_ROWS = 528

NEG_BIG = -1e30
INT_MIN = -(2 ** 31)

VMEM_LIMIT = 56 * 1024 * 1024


def _rms(x, g):
    ms = jnp.mean(x * x, axis=-1, keepdims=True)
    return x * lax.rsqrt(ms + NORM_EPS) * g


def _dot(a, b):
    return jnp.dot(a, b, preferred_element_type=F32)


def _dot_nt(a, b):
    return lax.dot_general(a, b, (((1,), (1,)), ((), ())), preferred_element_type=F32)


def _params(*sem):
    return pltpu.CompilerParams(dimension_semantics=sem, vmem_limit_bytes=VMEM_LIMIT)


def _in_proj_rows_kernel(x_ref, g_ref, w_ref, o_ref, h_ref):
    @pl.when(pl.program_id(1) == 0)
    def _():
        h_ref[...] = _rms(x_ref[...], g_ref[...]).astype(BF16)

    o_ref[...] = _dot(h_ref[...], w_ref[...])


def in_proj_rows(x, gain, w, *, tm=512, tn=512):
    t, d = x.shape
    n = w.shape[1]
    return pl.pallas_call(
        _in_proj_rows_kernel,
        out_shape=jax.ShapeDtypeStruct((t, n), F32),
        grid=(t // tm, n // tn),
        in_specs=[
            pl.BlockSpec((tm, d), lambda i, j: (i, 0)),
            pl.BlockSpec((1, d), lambda i, j: (0, 0)),
            pl.BlockSpec((d, tn), lambda i, j: (0, j)),
        ],
        out_specs=pl.BlockSpec((tm, tn), lambda i, j: (i, j)),
        scratch_shapes=[pltpu.VMEM((tm, d), BF16)],
        compiler_params=_params("parallel", "arbitrary"),
        name="in_proj_rows",
    )(x, gain, w)


def _in_proj_cols_kernel(x_ref, g_ref, wt_ref, o_ref):
    h = _rms(x_ref[...], g_ref[...]).astype(BF16)
    o_ref[...] = _dot_nt(wt_ref[...], h)


def in_proj_cols(x, gain, wt, *, tm=512):
    t, d = x.shape
    c = wt.shape[0]
    return pl.pallas_call(
        _in_proj_cols_kernel,
        out_shape=jax.ShapeDtypeStruct((c, t), F32),
        grid=(t // tm,),
        in_specs=[
            pl.BlockSpec((tm, d), lambda i: (i, 0)),
            pl.BlockSpec((1, d), lambda i: (0, 0)),
            pl.BlockSpec((c, d), lambda i: (0, 0)),
        ],
        out_specs=pl.BlockSpec((c, tm), lambda i: (0, i)),
        compiler_params=_params("parallel"),
        name="in_proj_cols",
    )(x, gain, wt)


def _pool_kernel(u_ref, mixw_ref, scale_ref, o_ref, buf_ref):
    s = u_ref.shape[0]
    buf_ref[0:POOL_HALO, :] = jnp.zeros((POOL_HALO, POOL_WIDTH), F32)
    buf_ref[POOL_HALO:POOL_HALO + s, :] = u_ref[...]
    pos = lax.broadcasted_iota(I32, (s, POOL_GROUP), 0).astype(F32)
    for gi, w in enumerate(POOL_WINDOWS):
        c0 = gi * POOL_GROUP
        tok = buf_ref[POOL_HALO:POOL_HALO + s, c0:c0 + POOL_GROUP]
        acc = tok
        for k in range(1, w):
            acc = acc + buf_ref[POOL_HALO - k:POOL_HALO - k + s, c0:c0 + POOL_GROUP]
        count = jnp.minimum(pos + 1.0, float(w))
        d = acc / count - tok
        y = _dot(d.astype(BF16), mixw_ref[gi])
        o_ref[:, c0:c0 + POOL_GROUP] = (y * scale_ref[:, c0:c0 + POOL_GROUP]).astype(BF16)


def pool_mixer(zrow, mix_w, scale):
    b, s, _ = zrow.shape
    return pl.pallas_call(
        _pool_kernel,
        out_shape=jax.ShapeDtypeStruct((b, s, POOL_WIDTH), BF16),
        grid=(b,),
        in_specs=[
            pl.BlockSpec((None, s, POOL_WIDTH), lambda i: (i, 0, COL_POOL // POOL_WIDTH)),
            pl.BlockSpec((len(POOL_WINDOWS), POOL_GROUP, POOL_GROUP), lambda i: (0, 0, 0)),
            pl.BlockSpec((1, POOL_WIDTH), lambda i: (0, 0)),
        ],
        out_specs=pl.BlockSpec((None, s, POOL_WIDTH), lambda i: (i, 0, 0)),
        scratch_shapes=[pltpu.VMEM((POOL_HALO + s, POOL_WIDTH), F32)],
        compiler_params=_params("parallel"),
        name="pool_mixer",
    )(zrow, mix_w, scale)


def _s5_disc_kernel(lr_ref, li_ref, ldt_ref, br_ref, bi_ref, ar_ref, ai_ref, bbr_ref, bbi_ref):
    lr = lr_ref[...]
    li = li_ref[...]
    dt = jnp.exp(ldt_ref[...])
    mag = jnp.exp(lr * dt)
    a_re = mag * jnp.cos(li * dt)
    a_im = mag * jnp.sin(li * dt)
    den = lr * lr + li * li
    coef_re = ((a_re - 1.0) * lr + a_im * li) / den
    coef_im = (a_im * lr - (a_re - 1.0) * li) / den
    ar_ref[...] = a_re
    ai_ref[...] = a_im
    br = br_ref[...]
    bi = bi_ref[...]
    bbr_ref[...] = coef_re * br - coef_im * bi
    bbi_ref[...] = coef_re * bi + coef_im * br


def s5_discretise(lam_re, lam_im, log_dt, b_re_t, b_im_t):
    g, _, n = lam_re.shape
    j = b_re_t.shape[1]
    return pl.pallas_call(
        _s5_disc_kernel,
        out_shape=(
            jax.ShapeDtypeStruct((g, 1, n), F32),
            jax.ShapeDtypeStruct((g, 1, n), F32),
            jax.ShapeDtypeStruct((g, j, n), F32),
            jax.ShapeDtypeStruct((g, j, n), F32),
        ),
        name="s5_discretise",
    )(lam_re, lam_im, log_dt, b_re_t, b_im_t)


def _s5_scan_kernel(u_ref, bre_ref, bim_ref, cre_ref, cim_ref, ar_ref, ai_ref, d_ref, o_ref,
                    ut_ref, xr_ref, xi_ref, sr_ref, si_ref, y_ref, *, lane_chunk):
    nb, L, _ = u_ref.shape

    @pl.when(pl.program_id(0) == 0)
    def _():
        sr_ref[...] = jnp.zeros_like(sr_ref)
        si_ref[...] = jnp.zeros_like(si_ref)

    n_slab = S5_WIDTH // 128
    for b in range(nb):
        for c in range(n_slab):
            ut_ref[c, pl.ds(b, L, stride=nb), :] = u_ref[b, :, c * 128:(c + 1) * 128]
    u_t = jnp.concatenate([ut_ref[c] for c in range(n_slab)], axis=1)
    ub = u_t.astype(BF16)
    xr_ref[...] = _dot(ub, bre_ref[...])
    xi_ref[...] = _dot(ub, bim_ref[...])

    for c in range(S5_LANES // lane_chunk):
        cs = slice(c * lane_chunk, (c + 1) * lane_chunk)
        a_re = jnp.broadcast_to(ar_ref[:, cs], (nb, lane_chunk))
        a_im = jnp.broadcast_to(ai_ref[:, cs], (nb, lane_chunk))

        def step(t, carry, cs=cs, a_re=a_re, a_im=a_im):
            s_re, s_im = carry
            r0 = pl.multiple_of(t * nb, nb)
            n_re = a_re * s_re - a_im * s_im + xr_ref[pl.ds(r0, nb), cs]
            n_im = a_re * s_im + a_im * s_re + xi_ref[pl.ds(r0, nb), cs]
            xr_ref[pl.ds(r0, nb), cs] = n_re
            xi_ref[pl.ds(r0, nb), cs] = n_im
            return n_re, n_im

        s_re, s_im = lax.fori_loop(0, L, step, (sr_ref[:, cs], si_ref[:, cs]), unroll=2)
        sr_ref[:, cs] = s_re
        si_ref[:, cs] = s_im

    y = _dot(xr_ref[...].astype(BF16), cre_ref[...]) - _dot(xi_ref[...].astype(BF16), cim_ref[...])
    y = jax.nn.gelu(y + d_ref[...] * u_t)
    for c in range(n_slab):
        y_ref[c] = y[:, c * 128:(c + 1) * 128]
    for b in range(nb):
        for c in range(n_slab):
            o_ref[b, :, c * 128:(c + 1) * 128] = y_ref[c, pl.ds(b, L, stride=nb), :].astype(BF16)


def s5_scan(zrow, b_re_full, b_im_full, c_re_full, c_im_full, a_re, a_im, d_skip, *, chunk=64, lane_chunk=1024):
    b, s, _ = zrow.shape
    rows = chunk * b
    const = lambda i: (0, 0)
    return pl.pallas_call(
        functools.partial(_s5_scan_kernel, lane_chunk=lane_chunk),
        out_shape=jax.ShapeDtypeStruct((b, s, S5_WIDTH), BF16),
        grid=(s // chunk,),
        in_specs=[
            pl.BlockSpec((b, chunk, S5_WIDTH), lambda i: (0, i, COL_S5 // S5_WIDTH)),
            pl.BlockSpec((S5_WIDTH, S5_LANES), const),
            pl.BlockSpec((S5_WIDTH, S5_LANES), const),
            pl.BlockSpec((S5_LANES, S5_WIDTH), const),
            pl.BlockSpec((S5_LANES, S5_WIDTH), const),
            pl.BlockSpec((1, S5_LANES), const),
            pl.BlockSpec((1, S5_LANES), const),
            pl.BlockSpec((1, S5_WIDTH), const),
        ],
        out_specs=pl.BlockSpec((b, chunk, S5_WIDTH), lambda i: (0, i, 0)),
        scratch_shapes=[
            pltpu.VMEM((S5_WIDTH // 128, rows, 128), F32),
            pltpu.VMEM((rows, S5_LANES), F32),
            pltpu.VMEM((rows, S5_LANES), F32),
            pltpu.VMEM((b, S5_LANES), F32),
            pltpu.VMEM((b, S5_LANES), F32),
            pltpu.VMEM((S5_WIDTH // 128, rows, 128), F32),
        ],
        compiler_params=_params("arbitrary"),
        name="s5_scan",
    )(zrow, b_re_full, b_im_full, c_re_full, c_im_full, a_re, a_im, d_skip)


def _rel_bias_kernel(rb_ref, o_ref):
    p = pl.program_id(0)
    h = pl.program_id(1)
    ql = lax.broadcasted_iota(I32, (Q_BLOCK, 2 * Q_BLOCK), 0)
    c = lax.broadcasted_iota(I32, (Q_BLOCK, 2 * Q_BLOCK), 1)
    dist = jnp.maximum(ql - c + p * Q_BLOCK, 0)
    max_exact = REL_BUCKETS // 2
    d_f = jnp.maximum(dist, 1).astype(F32)
    large = max_exact + (jnp.log(d_f / max_exact) / math.log(REL_MAX_DISTANCE / max_exact)
                         * (REL_BUCKETS - max_exact)).astype(I32)
    large = jnp.minimum(large, REL_BUCKETS - 1)
    bucket = jnp.where(dist < max_exact, dist, large)
    acc = jnp.zeros((Q_BLOCK, 2 * Q_BLOCK), F32)
    for k in range(REL_BUCKETS):
        acc = jnp.where(bucket == k, rb_ref[k, h], acc)
    o_ref[...] = acc - rb_ref[REL_BUCKETS - 1, h]


def rel_bias_tiles(rel_bias):
    return pl.pallas_call(
        _rel_bias_kernel,
        out_shape=jax.ShapeDtypeStruct((2, ATTN_HEADS, Q_BLOCK, 2 * Q_BLOCK), F32),
        grid=(2, ATTN_HEADS),
        in_specs=[pl.BlockSpec(memory_space=pltpu.SMEM)],
        out_specs=pl.BlockSpec((None, None, Q_BLOCK, 2 * Q_BLOCK), lambda p, h: (p, h, 0, 0)),
        name="rel_bias_tiles",
    )(rel_bias)


def _count(mask):
    nt, nk, nq = mask.shape
    n = nt * nk // 8
    x = jnp.where(mask, 1.0, 0.0).reshape(n, 8, nq)
    accs = [x[c] for c in range(COUNT_CHAINS)]
    for j in range(COUNT_CHAINS, n):
        accs[j % COUNT_CHAINS] = accs[j % COUNT_CHAINS] + x[j]
    while len(accs) > 1:
        accs = [accs[c] + accs[c + len(accs) // 2] for c in range(len(accs) // 2)]
    return jnp.sum(accs[0], axis=0, keepdims=True)


COUNT_CHAINS = 8


KEY_NEG_INF = -2139095041
NO_TIE_POS = 2 ** 30
AUG_DEPTH = 256


def _dsa_kernel(q_ref, k_ref, v_ref, ki_ref, qit_ref, wit_ref, bias_ref, o_ref,
                kaug_ref, vb_ref, kib_ref, qaug_ref, key_ref, tie_ref, mask_ref, oh_ref, *, qb0, topk, hp):
    sc = k_ref.shape[0]
    nt = sc // Q_BLOCK
    rows = hp * Q_BLOCK
    i = pl.program_id(1) + qb0
    n0 = jnp.maximum(i - 1, 0)

    @pl.when(pl.program_id(1) == 0)
    def _():
        kaug_ref[:, Q_BLOCK:Q_BLOCK + ATTN_HEAD_DIM] = k_ref[:, 0:ATTN_HEAD_DIM].astype(BF16)
        kaug_ref[:, Q_BLOCK + ATTN_HEAD_DIM:] = jnp.zeros((sc, AUG_DEPTH - Q_BLOCK - ATTN_HEAD_DIM), BF16)
        vb_ref[...] = v_ref[:, 0:ATTN_HEAD_DIM].astype(BF16)
        kib_ref[...] = ki_ref[:, 0:IDX_HEAD_DIM].astype(BF16)
        r = lax.broadcasted_iota(I32, (rows, Q_BLOCK), 0)
        c = lax.broadcasted_iota(I32, (rows, Q_BLOCK), 1)
        eye = jnp.where(jnp.bitwise_and(r, Q_BLOCK - 1) == c, 1.0, 0.0).astype(BF16)
        for g in range(ATTN_HEADS // hp):
            qaug_ref[g, :, 0:Q_BLOCK] = eye
            qaug_ref[g, :, Q_BLOCK + ATTN_HEAD_DIM:] = jnp.zeros((rows, AUG_DEPTH - Q_BLOCK - ATTN_HEAD_DIM), BF16)

    kib = kib_ref[...]
    score = jnp.zeros((sc, Q_BLOCK), F32)
    for h in range(IDX_HEADS):
        qh = qit_ref[h * IDX_HEAD_DIM:(h + 1) * IDX_HEAD_DIM, :].astype(BF16)
        rel = jnp.maximum(_dot(kib, qh), 0.0)
        score = score + rel * wit_ref[h:h + 1, :]
    score = score * ((IDX_HEADS ** -0.5) * (IDX_HEAD_DIM ** -0.5))
    kpos2 = lax.broadcasted_iota(I32, (sc, Q_BLOCK), 0)
    qpos2 = lax.broadcasted_iota(I32, (sc, Q_BLOCK), 1) + i * Q_BLOCK
    score = jnp.where(kpos2 <= qpos2, score + 0.0, -jnp.inf)
    bits = pltpu.bitcast(score, I32)
    key2 = jnp.where(bits < 0, bits ^ 0x7FFFFFFF, bits)
    key_ref[...] = key2.reshape(nt, Q_BLOCK, Q_BLOCK)

    def value_step(t, thr):
        cand = thr + jnp.left_shift(jnp.int32(1), 31 - t)
        cnt = _count(key_ref[...] >= cand[None])
        return jnp.where(cnt >= topk, cand, thr)

    thr = lax.fori_loop(0, 32, value_step, jnp.full((1, Q_BLOCK), INT_MIN, I32))

    shape3 = (nt, Q_BLOCK, Q_BLOCK)
    kpos = lax.broadcasted_iota(I32, shape3, 0) * Q_BLOCK + lax.broadcasted_iota(I32, shape3, 1)
    qpos = lax.broadcasted_iota(I32, shape3, 2) + i * Q_BLOCK
    n_gt = _count(key_ref[...] > thr[None])
    n_ge = _count(key_ref[...] >= thr[None])
    need = topk - n_gt
    tie_ref[...] = jnp.where(key_ref[...] == thr[None], kpos, jnp.int32(NO_TIE_POS))
    excess = jnp.where(jnp.logical_and(n_ge > topk, thr != KEY_NEG_INF), 1, 0)

    def index_search():
        def index_step(t, m):
            cand = m + jnp.left_shift(jnp.int32(1), 10 - t)
            cnt = _count(tie_ref[...] < cand[None])
            return jnp.where(cnt < need, cand, m)

        return lax.fori_loop(0, 11, index_step, jnp.zeros((1, Q_BLOCK), I32))

    m = lax.cond(jnp.max(excess) > 0, index_search, lambda: jnp.full((1, Q_BLOCK), NO_TIE_POS - 1, I32))
    causal_bias = jnp.where(kpos <= qpos, 0.0, NEG_BIG)
    tie_bias = jnp.where(tie_ref[...] <= m[None], causal_bias, NEG_BIG)
    mask3 = jnp.where(key_ref[...] > thr[None], causal_bias, tie_bias)
    far3 = jnp.where(kpos < n0 * Q_BLOCK, mask3, NEG_BIG)
    mask_ref[...] = mask3.reshape(sc, Q_BLOCK).astype(BF16)
    kaug_ref[:, 0:Q_BLOCK] = far3.reshape(sc, Q_BLOCK).astype(BF16)

    q = q_ref[...] * (ATTN_HEAD_DIM ** -0.5)
    for h in range(ATTN_HEADS):
        r0 = (h % hp) * Q_BLOCK
        qaug_ref[h // hp, r0:r0 + Q_BLOCK, Q_BLOCK:Q_BLOCK + ATTN_HEAD_DIM] = (
            q[:, h * ATTN_HEAD_DIM:(h + 1) * ATTN_HEAD_DIM].astype(BF16))

    near0 = pl.multiple_of(n0 * Q_BLOCK, Q_BLOCK)
    kaug_near = jnp.concatenate(
        [mask_ref[pl.ds(near0, 2 * Q_BLOCK), :], kaug_ref[pl.ds(near0, 2 * Q_BLOCK), Q_BLOCK:]], axis=1)
    v_near = vb_ref[pl.ds(near0, 2 * Q_BLOCK), :]
    pidx = jnp.minimum(i, 1)

    def head_group(g, carry):
        qa = qaug_ref[g]
        lf = _dot_nt(qa, kaug_ref[...])
        ln = _dot_nt(qa, kaug_near) + bias_ref[pidx, g]
        mx = jnp.maximum(jnp.max(lf, axis=1, keepdims=True), jnp.max(ln, axis=1, keepdims=True))
        pf = jnp.exp(lf - mx)
        pn = jnp.exp(ln - mx)
        den = jnp.sum(pf, axis=1, keepdims=True) + jnp.sum(pn, axis=1, keepdims=True)
        acc = _dot(pf.astype(BF16), vb_ref[...]) + _dot(pn.astype(BF16), v_near)
        oh_ref[g] = acc / den
        return carry

    lax.fori_loop(0, ATTN_HEADS // hp, head_group, 0, unroll=2)
    for h in range(ATTN_HEADS):
        r0 = (h % hp) * Q_BLOCK
        o_ref[:, h * ATTN_HEAD_DIM:(h + 1) * ATTN_HEAD_DIM] = oh_ref[h // hp, r0:r0 + Q_BLOCK, :]


def dsa_attention(zrow, idx_t, bias_tiles, *, qb0, qb1, hp=2):
    b, s, _ = zrow.shape
    sc = qb1 * Q_BLOCK
    assert sc <= 2048, "the tie-break index search covers 11 bits"
    nq = qb1 - qb0
    nqs = s // Q_BLOCK
    ng = ATTN_HEADS // hp
    rows = hp * Q_BLOCK
    topk = min(TOPK_MAX, s // TOPK_DIVISOR)
    kv_spec = lambda col: pl.BlockSpec((None, sc, 128), lambda bi, qi: (bi, 0, col // 128))
    return pl.pallas_call(
        functools.partial(_dsa_kernel, qb0=qb0, topk=topk, hp=hp),
        out_shape=jax.ShapeDtypeStruct((b, nq * Q_BLOCK, ATTN_WIDTH), F32),
        grid=(b, nq),
        in_specs=[
            pl.BlockSpec((None, Q_BLOCK, ATTN_WIDTH), lambda bi, qi: (bi, qi + qb0, COL_Q // ATTN_WIDTH)),
            kv_spec(COL_K),
            kv_spec(COL_V),
            kv_spec(COL_KI),
            pl.BlockSpec((IDX_HEADS * IDX_HEAD_DIM, Q_BLOCK), lambda bi, qi: (0, bi * nqs + qi + qb0)),
            pl.BlockSpec((16, Q_BLOCK), lambda bi, qi: (IDX_HEADS * IDX_HEAD_DIM // 16, bi * nqs + qi + qb0)),
            pl.BlockSpec((2, ng, rows, 2 * Q_BLOCK), lambda bi, qi: (0, 0, 0, 0)),
        ],
        out_specs=pl.BlockSpec((None, Q_BLOCK, ATTN_WIDTH), lambda bi, qi: (bi, qi, 0)),
        scratch_shapes=[
            pltpu.VMEM((sc, AUG_DEPTH), BF16),
            pltpu.VMEM((sc, ATTN_HEAD_DIM), BF16),
            pltpu.VMEM((sc, IDX_HEAD_DIM), BF16),
            pltpu.VMEM((ng, rows, AUG_DEPTH), BF16),
            pltpu.VMEM((sc // Q_BLOCK, Q_BLOCK, Q_BLOCK), I32),
            pltpu.VMEM((sc // Q_BLOCK, Q_BLOCK, Q_BLOCK), I32),
            pltpu.VMEM((sc, Q_BLOCK), BF16),
            pltpu.VMEM((ng, rows, ATTN_HEAD_DIM), F32),
        ],
        compiler_params=_params("parallel", "arbitrary"),
        name="dsa_attention",
    )(zrow, zrow, zrow, zrow, idx_t, idx_t, bias_tiles.reshape(2, ng, rows, 2 * Q_BLOCK))


def _merge_kernel(x_ref, gpre_ref, wg_ref, pool_ref, wpool_ref, attn_ref, wattn_ref, s5_ref, wglu_ref,
                  wout_ref, gpost_ref, o_ref):
    x = x_ref[...]
    h = _rms(x, gpre_ref[...]).astype(BF16)
    gates = jax.nn.sigmoid(_dot(h, wg_ref[...]))
    y_pool = _dot(pool_ref[...], wpool_ref[...])
    y_attn = _dot(attn_ref[...].astype(BF16), wattn_ref[...])
    glu = _dot(s5_ref[...], wglu_ref[...])
    y_s5 = glu[:, :D_MODEL] * jax.nn.sigmoid(glu[:, D_MODEL:])
    merged = (gates[:, 0:D_MODEL] * y_pool + gates[:, D_MODEL:2 * D_MODEL] * y_attn
              + gates[:, 2 * D_MODEL:3 * D_MODEL] * y_s5)
    mixed = _dot(merged.astype(BF16), wout_ref[...])
    o_ref[...] = x + _rms(mixed, gpost_ref[...])


def merge(x, g_pre, w_gates, pool_y, w_pool, attn_y, w_attn, s5_y, w_glu, w_out, g_post, *, tm=256):
    t, d = x.shape
    row = lambda width: pl.BlockSpec((tm, width), lambda i: (i, 0))
    full = lambda a: pl.BlockSpec(a.shape, lambda i: (0, 0))
    return pl.pallas_call(
        _merge_kernel,
        out_shape=jax.ShapeDtypeStruct((t, d), F32),
        grid=(t // tm,),
        in_specs=[row(d), full(g_pre), full(w_gates), row(POOL_WIDTH), full(w_pool), row(ATTN_WIDTH),
                  full(w_attn), row(S5_WIDTH), full(w_glu), full(w_out), full(g_post)],
        out_specs=row(d),
        compiler_params=_params("parallel"),
        name="merge",
    )(x, g_pre, w_gates, pool_y, w_pool, attn_y, w_attn, s5_y, w_glu, w_out, g_post)


def _ffn_kernel(x_ref, gpre_ref, wgate_ref, wup_ref, wdown_ref, gpost_ref, o_ref, h_ref, acc_ref):
    j = pl.program_id(1)

    @pl.when(j == 0)
    def _():
        h_ref[...] = _rms(x_ref[...], gpre_ref[...]).astype(BF16)
        acc_ref[...] = jnp.zeros_like(acc_ref)

    h = h_ref[...]
    gate = _dot(h, wgate_ref[...])
    up = _dot(h, wup_ref[...])
    act = (jax.nn.silu(gate) * up).astype(BF16)
    acc_ref[...] += _dot(act, wdown_ref[...])

    @pl.when(j == pl.num_programs(1) - 1)
    def _():
        o_ref[...] = x_ref[...] + _rms(acc_ref[...], gpost_ref[...])


def ffn(x, g_pre, w_in, w_down, g_post, *, tm=512, th=256):
    t, d = x.shape
    hid = w_down.shape[0]
    nh = hid // th
    return pl.pallas_call(
        _ffn_kernel,
        out_shape=jax.ShapeDtypeStruct((t, d), F32),
        grid=(t // tm, nh),
        in_specs=[
            pl.BlockSpec((tm, d), lambda i, j: (i, 0)),
            pl.BlockSpec((1, d), lambda i, j: (0, 0)),
            pl.BlockSpec((d, th), lambda i, j: (0, j)),
            pl.BlockSpec((d, th), lambda i, j: (0, j + nh)),
            pl.BlockSpec((th, d), lambda i, j: (j, 0)),
            pl.BlockSpec((1, d), lambda i, j: (0, 0)),
        ],
        out_specs=pl.BlockSpec((tm, d), lambda i, j: (i, 0)),
        scratch_shapes=[pltpu.VMEM((tm, d), BF16), pltpu.VMEM((tm, d), F32)],
        compiler_params=_params("parallel", "arbitrary"),
        name="ffn",
    )(x, g_pre, w_in, w_in, w_down, g_post)


def _pad_cols(w, width):
    return jnp.pad(w, ((0, 0), (0, width - w.shape[1])))


def _split_w_in(w):
    sizes = (POOL_WIDTH, ATTN_WIDTH, ATTN_HEAD_DIM, ATTN_HEAD_DIM, IDX_HEADS * IDX_HEAD_DIM, IDX_HEAD_DIM,
             IDX_HEADS, S5_WIDTH, N_BRANCHES * D_MODEL)
    parts, start = [], 0
    for size in sizes:
        parts.append(w[:, start:start + size])
        start += size
    w_pool, w_q, w_k, w_v, w_qi, w_ki, w_wi, w_s5, w_gates = parts
    w_rows = jnp.concatenate(
        [w_pool, w_s5, w_q, _pad_cols(w_k, 128), _pad_cols(w_v, 128), _pad_cols(w_ki, ROW_WIDTH - COL_KI)], axis=1)
    w_cols = jnp.concatenate([w_qi, _pad_cols(w_wi, IDX_T_ROWS - IDX_HEADS * IDX_HEAD_DIM)], axis=1).T
    return w_rows.astype(BF16), w_cols.astype(BF16), w_gates.astype(BF16)


def _block_diag(blocks):
    g, r, c = blocks.shape
    eye = jnp.eye(g, dtype=blocks.dtype)
    return (blocks[:, :, None, :] * eye[:, None, :, None]).reshape(g * r, g * c)


def _attn_classes(nqs):
    n_classes = 4 if nqs % 4 == 0 and nqs >= 8 else 1
    step = nqs // n_classes
    return [(c * step, (c + 1) * step) for c in range(n_classes)]


def kernel(x, norm_mix_pre, norm_mix_post, norm_ffn_pre, norm_ffn_post, w_in, pool_mix_w, pool_scale, pool_out_w, attn_out_w, rel_bias, s5_lambda_re, s5_lambda_im, s5_log_dt, s5_b_re, s5_b_im, s5_c_re, s5_c_im, s5_d, s5_glu_w, w_out, ffn_w_in, ffn_w_out):
    b, s, d = x.shape
    t = b * s
    depth = w_in.shape[0]
    bias_tiles = rel_bias_tiles(rel_bias)
    xf = x.reshape(t, d)
    for l in range(depth):
        w_rows, w_cols, w_gates = _split_w_in(w_in[l])
        g_pre = norm_mix_pre[l][None, :]
        zrow = in_proj_rows(xf, g_pre, w_rows).reshape(b, s, ROW_WIDTH)
        idx_t = in_proj_cols(xf, g_pre, w_cols)

        pool_y = pool_mixer(zrow, pool_mix_w[l].astype(BF16), pool_scale[l][None, :])

        a_re, a_im, bb_re, bb_im = s5_discretise(
            s5_lambda_re[l][:, None, :], s5_lambda_im[l][:, None, :], s5_log_dt[l][:, None, None],
            jnp.swapaxes(s5_b_re[l], 1, 2), jnp.swapaxes(s5_b_im[l], 1, 2))
        s5_y = s5_scan(
            zrow,
            _block_diag(bb_re).astype(BF16), _block_diag(bb_im).astype(BF16),
            _block_diag(jnp.swapaxes(s5_c_re[l], 1, 2)).astype(BF16),
            _block_diag(jnp.swapaxes(s5_c_im[l], 1, 2)).astype(BF16),
            a_re.reshape(1, S5_LANES), a_im.reshape(1, S5_LANES), s5_d[l][None, :])

        attn_y = jnp.concatenate(
            [dsa_attention(zrow, idx_t, bias_tiles, qb0=q0, qb1=q1)
             for q0, q1 in _attn_classes(s // Q_BLOCK)], axis=1)

        xf = merge(xf, g_pre, w_gates, pool_y.reshape(t, POOL_WIDTH), pool_out_w[l].astype(BF16),
                   attn_y.reshape(t, ATTN_WIDTH), attn_out_w[l].astype(BF16),
                   s5_y.reshape(t, S5_WIDTH), s5_glu_w[l].astype(BF16), w_out[l].astype(BF16),
                   norm_mix_post[l][None, :])
        xf = ffn(xf, norm_ffn_pre[l][None, :], ffn_w_in[l].astype(BF16), ffn_w_out[l].astype(BF16),
                 norm_ffn_post[l][None, :])
    return xf.reshape(b, s, d)
```

```python
import functools
import math

import jax
import jax.numpy as jnp
from jax import lax
from jax.experimental import pallas as pl
from jax.experimental.pallas import tpu as pltpu

F32 = jnp.float32
BF16 = jnp.bfloat16
I32 = jnp.int32

D_MODEL = 1024
POOL_WINDOWS = (2, 4, 8, 16)
POOL_WIDTH = 512
POOL_GROUP = 128
POOL_HALO = 16
ATTN_HEADS = 16
ATTN_HEAD_DIM = 64
ATTN_WIDTH = 1024
IDX_HEADS = 8
IDX_HEAD_DIM = 64
TOPK_MAX = 256
TOPK_DIVISOR = 4
Q_BLOCK = 128
REL_BUCKETS = 32
REL_MAX_DISTANCE = 128
S5_WIDTH = 512
S5_GROUP = 16
S5_GROUPS = 32
S5_STATE = 64
S5_LANES = S5_GROUPS * S5_STATE
N_BRANCHES = 3
FFN_HIDDEN = 2816
NORM_EPS = 1e-6

COL_POOL = 0
COL_S5 = 512
COL_Q = 1024
COL_K = 2048
COL_V = 2176
COL_KI = 2304
ROW_WIDTH = 2560
IDX_T_ROWS = 528

NEG_BIG = -1e30
INT_MIN = -(2 ** 31)

VMEM_LIMIT = 56 * 1024 * 1024


def _rms(x, g):
    ms = jnp.mean(x * x, axis=-1, keepdims=True)
    return x * lax.rsqrt(ms + NORM_EPS) * g


def _dot(a, b):
    return jnp.dot(a, b, preferred_element_type=F32)


def _dot_nt(a, b):
    return lax.dot_general(a, b, (((1,), (1,)), ((), ())), preferred_element_type=F32)


def _params(*sem):
    return pltpu.CompilerParams(dimension_semantics=sem, vmem_limit_bytes=VMEM_LIMIT)


def _resident(shape):
    return pl.BlockSpec(shape, lambda *_: (0,) * len(shape), pipeline_mode=pl.Buffered(1))


def _in_proj_kernel(x_ref, g_ref, w_ref, wt_ref, rows_ref, cols_ref):
    h = _rms(x_ref[...], g_ref[...]).astype(BF16)
    rows_ref[...] = _dot(h, w_ref[...])
    cols_ref[...] = _dot_nt(wt_ref[...], h)


def in_proj(x, gain, w, wt, *, tm=512):
    t, d = x.shape
    n = w.shape[1]
    c = wt.shape[0]
    return pl.pallas_call(
        _in_proj_kernel,
        out_shape=(jax.ShapeDtypeStruct((t, n), F32), jax.ShapeDtypeStruct((c, t), F32)),
        grid=(t // tm,),
        in_specs=[
            pl.BlockSpec((tm, d), lambda i: (i, 0)),
            _resident((1, d)),
            _resident((d, n)),
            _resident((c, d)),
        ],
        out_specs=(pl.BlockSpec((tm, n), lambda i: (i, 0)), pl.BlockSpec((c, tm), lambda i: (0, i))),
        compiler_params=_params("parallel"),
        name="in_proj",
    )(x, gain, w, wt)


def _pool_kernel(u_ref, mixw_ref, scale_ref, o_ref, buf_ref):
    s = u_ref.shape[0]
    buf_ref[0:POOL_HALO, :] = jnp.zeros((POOL_HALO, POOL_WIDTH), F32)
    buf_ref[POOL_HALO:POOL_HALO + s, :] = u_ref[...]
    pos = lax.broadcasted_iota(I32, (s, POOL_GROUP), 0).astype(F32)
    for gi, w in enumerate(POOL_WINDOWS):
        c0 = gi * POOL_GROUP
        tok = buf_ref[POOL_HALO:POOL_HALO + s, c0:c0 + POOL_GROUP]
        acc = tok
        for k in range(1, w):
            acc = acc + buf_ref[POOL_HALO - k:POOL_HALO - k + s, c0:c0 + POOL_GROUP]
        count = jnp.minimum(pos + 1.0, float(w))
        d = acc / count - tok
        y = _dot(d.astype(BF16), mixw_ref[gi])
        o_ref[:, c0:c0 + POOL_GROUP] = (y * scale_ref[:, c0:c0 + POOL_GROUP]).astype(BF16)


def pool_mixer(zrow, mix_w, scale):
    b, s, _ = zrow.shape
    return pl.pallas_call(
        _pool_kernel,
        out_shape=jax.ShapeDtypeStruct((b, s, POOL_WIDTH), BF16),
        grid=(b,),
        in_specs=[
            pl.BlockSpec((None, s, POOL_WIDTH), lambda i: (i, 0, COL_POOL // POOL_WIDTH)),
            pl.BlockSpec((len(POOL_WINDOWS), POOL_GROUP, POOL_GROUP), lambda i: (0, 0, 0)),
            pl.BlockSpec((1, POOL_WIDTH), lambda i: (0, 0)),
        ],
        out_specs=pl.BlockSpec((None, s, POOL_WIDTH), lambda i: (i, 0, 0)),
        scratch_shapes=[pltpu.VMEM((POOL_HALO + s, POOL_WIDTH), F32)],
        compiler_params=_params("parallel"),
        name="pool_mixer",
    )(zrow, mix_w, scale)


def _s5_disc_kernel(lr_ref, li_ref, ldt_ref, br_ref, bi_ref, ar_ref, ai_ref, bbr_ref, bbi_ref):
    lr = lr_ref[...]
    li = li_ref[...]
    dt = jnp.exp(ldt_ref[...])
    mag = jnp.exp(lr * dt)
    a_re = mag * jnp.cos(li * dt)
    a_im = mag * jnp.sin(li * dt)
    den = lr * lr + li * li
    coef_re = ((a_re - 1.0) * lr + a_im * li) / den
    coef_im = (a_im * lr - (a_re - 1.0) * li) / den
    ar_ref[...] = a_re
    ai_ref[...] = a_im
    br = br_ref[...]
    bi = bi_ref[...]
    bbr_ref[...] = coef_re * br - coef_im * bi
    bbi_ref[...] = coef_re * bi + coef_im * br


def s5_discretise(lam_re, lam_im, log_dt, b_re_t, b_im_t):
    g, _, n = lam_re.shape
    j = b_re_t.shape[1]
    return pl.pallas_call(
        _s5_disc_kernel,
        out_shape=(
            jax.ShapeDtypeStruct((g, 1, n), F32),
            jax.ShapeDtypeStruct((g, 1, n), F32),
            jax.ShapeDtypeStruct((g, j, n), F32),
            jax.ShapeDtypeStruct((g, j, n), F32),
        ),
        name="s5_discretise",
    )(lam_re, lam_im, log_dt, b_re_t, b_im_t)


def _s5_scan_kernel(u_ref, bre_ref, bim_ref, cre_ref, cim_ref, ar_ref, ai_ref, d_ref, o_ref,
                    ut_ref, xr_ref, xi_ref, sr_ref, si_ref, y_ref, *, lane_chunk):
    nb, L, _ = u_ref.shape

    @pl.when(pl.program_id(0) == 0)
    def _():
        sr_ref[...] = jnp.zeros_like(sr_ref)
        si_ref[...] = jnp.zeros_like(si_ref)

    n_slab = S5_WIDTH // 128
    for b in range(nb):
        for c in range(n_slab):
            ut_ref[c, pl.ds(b, L, stride=nb), :] = u_ref[b, :, c * 128:(c + 1) * 128]
    u_t = jnp.concatenate([ut_ref[c] for c in range(n_slab)], axis=1)
    ub = u_t.astype(BF16)
    n_blk = S5_WIDTH // 128
    lanes_blk = S5_LANES // n_blk
    for k in range(n_blk):
        rs = slice(k * 128, (k + 1) * 128)
        ls = slice(k * lanes_blk, (k + 1) * lanes_blk)
        xr_ref[:, ls] = _dot(ub[:, rs], bre_ref[rs, ls])
        xi_ref[:, ls] = _dot(ub[:, rs], bim_ref[rs, ls])

    for c in range(S5_LANES // lane_chunk):
        cs = slice(c * lane_chunk, (c + 1) * lane_chunk)
        a_re = jnp.broadcast_to(ar_ref[:, cs], (nb, lane_chunk))
        a_im = jnp.broadcast_to(ai_ref[:, cs], (nb, lane_chunk))

        def step(t, carry, cs=cs, a_re=a_re, a_im=a_im):
            s_re, s_im = carry
            r0 = pl.multiple_of(t * nb, nb)
            n_re = a_re * s_re - a_im * s_im + xr_ref[pl.ds(r0, nb), cs]
            n_im = a_re * s_im + a_im * s_re + xi_ref[pl.ds(r0, nb), cs]
            xr_ref[pl.ds(r0, nb), cs] = n_re
            xi_ref[pl.ds(r0, nb), cs] = n_im
            return n_re, n_im

        s_re, s_im = lax.fori_loop(0, L, step, (sr_ref[:, cs], si_ref[:, cs]), unroll=2)
        sr_ref[:, cs] = s_re
        si_ref[:, cs] = s_im

    for k in range(n_blk):
        rs = slice(k * 128, (k + 1) * 128)
        ls = slice(k * lanes_blk, (k + 1) * lanes_blk)
        y = _dot(xr_ref[:, ls].astype(BF16), cre_ref[ls, rs]) - _dot(xi_ref[:, ls].astype(BF16), cim_ref[ls, rs])
        y_ref[k] = jax.nn.gelu(y + d_ref[:, rs] * u_t[:, rs])
    for b in range(nb):
        for c in range(n_slab):
            o_ref[b, :, c * 128:(c + 1) * 128] = y_ref[c, pl.ds(b, L, stride=nb), :].astype(BF16)


def s5_scan(zrow, b_re_full, b_im_full, c_re_full, c_im_full, a_re, a_im, d_skip, *, chunk=64, lane_chunk=1024):
    b, s, _ = zrow.shape
    rows = chunk * b
    return pl.pallas_call(
        functools.partial(_s5_scan_kernel, lane_chunk=lane_chunk),
        out_shape=jax.ShapeDtypeStruct((b, s, S5_WIDTH), BF16),
        grid=(s // chunk,),
        in_specs=[
            pl.BlockSpec((b, chunk, S5_WIDTH), lambda i: (0, i, COL_S5 // S5_WIDTH)),
            _resident((S5_WIDTH, S5_LANES)),
            _resident((S5_WIDTH, S5_LANES)),
            _resident((S5_LANES, S5_WIDTH)),
            _resident((S5_LANES, S5_WIDTH)),
            _resident((1, S5_LANES)),
            _resident((1, S5_LANES)),
            _resident((1, S5_WIDTH)),
        ],
        out_specs=pl.BlockSpec((b, chunk, S5_WIDTH), lambda i: (0, i, 0)),
        scratch_shapes=[
            pltpu.VMEM((S5_WIDTH // 128, rows, 128), F32),
            pltpu.VMEM((rows, S5_LANES), F32),
            pltpu.VMEM((rows, S5_LANES), F32),
            pltpu.VMEM((b, S5_LANES), F32),
            pltpu.VMEM((b, S5_LANES), F32),
            pltpu.VMEM((S5_WIDTH // 128, rows, 128), F32),
        ],
        compiler_params=_params("arbitrary"),
        name="s5_scan",
    )(zrow, b_re_full, b_im_full, c_re_full, c_im_full, a_re, a_im, d_skip)


def _rel_bias_kernel(rb_ref, o_ref):
    p = pl.program_id(0)
    h = pl.program_id(1)
    ql = lax.broadcasted_iota(I32, (Q_BLOCK, 2 * Q_BLOCK), 0)
    c = lax.broadcasted_iota(I32, (Q_BLOCK, 2 * Q_BLOCK), 1)
    dist = jnp.maximum(ql - c + p * Q_BLOCK, 0)
    max_exact = REL_BUCKETS // 2
    d_f = jnp.maximum(dist, 1).astype(F32)
    large = max_exact + (jnp.log(d_f / max_exact) / math.log(REL_MAX_DISTANCE / max_exact)
                         * (REL_BUCKETS - max_exact)).astype(I32)
    large = jnp.minimum(large, REL_BUCKETS - 1)
    bucket = jnp.where(dist < max_exact, dist, large)
    acc = jnp.zeros((Q_BLOCK, 2 * Q_BLOCK), F32)
    for k in range(REL_BUCKETS):
        acc = jnp.where(bucket == k, rb_ref[k, h], acc)
    o_ref[...] = acc - rb_ref[REL_BUCKETS - 1, h]


def rel_bias_tiles(rel_bias):
    return pl.pallas_call(
        _rel_bias_kernel,
        out_shape=jax.ShapeDtypeStruct((2, ATTN_HEADS, Q_BLOCK, 2 * Q_BLOCK), F32),
        grid=(2, ATTN_HEADS),
        in_specs=[pl.BlockSpec(memory_space=pltpu.SMEM)],
        out_specs=pl.BlockSpec((None, None, Q_BLOCK, 2 * Q_BLOCK), lambda p, h: (p, h, 0, 0)),
        name="rel_bias_tiles",
    )(rel_bias)


def _count(mask):
    nt, nk, nq = mask.shape
    n = nt * nk // 8
    x = jnp.where(mask, 1.0, 0.0).reshape(n, 8, nq)
    accs = [x[c] for c in range(COUNT_CHAINS)]
    for j in range(COUNT_CHAINS, n):
        accs[j % COUNT_CHAINS] = accs[j % COUNT_CHAINS] + x[j]
    while len(accs) > 1:
        accs = [accs[c] + accs[c + len(accs) // 2] for c in range(len(accs) // 2)]
    return jnp.sum(accs[0], axis=0, keepdims=True)


COUNT_CHAINS = 8


KEY_NEG_INF = -2139095041
NO_TIE_POS = 2 ** 30
AUG_DEPTH = 256


def _dsa_kernel(q_ref, k_ref, v_ref, ki_ref, qit_ref, wit_ref, bias_ref, o_ref,
                kaug_ref, vb_ref, kib_ref, qaug_ref, key_ref, tie_ref, mask_ref, oh_ref, *, qb0, topk, hp):
    sc = k_ref.shape[0]
    nt = sc // Q_BLOCK
    rows = hp * Q_BLOCK
    i = pl.program_id(1) + qb0
    n0 = jnp.maximum(i - 1, 0)

    @pl.when(pl.program_id(1) == 0)
    def _():
        kaug_ref[:, Q_BLOCK:Q_BLOCK + ATTN_HEAD_DIM] = k_ref[:, 0:ATTN_HEAD_DIM].astype(BF16)
        kaug_ref[:, Q_BLOCK + ATTN_HEAD_DIM:] = jnp.zeros((sc, AUG_DEPTH - Q_BLOCK - ATTN_HEAD_DIM), BF16)
        lane = lax.broadcasted_iota(I32, (sc, 128), 1)
        vb_ref[...] = jnp.where(lane < ATTN_HEAD_DIM, v_ref[...], jnp.where(lane == ATTN_HEAD_DIM, 1.0, 0.0)).astype(BF16)
        kib_ref[...] = ki_ref[:, 0:IDX_HEAD_DIM].astype(BF16)
        r = lax.broadcasted_iota(I32, (rows, Q_BLOCK), 0)
        c = lax.broadcasted_iota(I32, (rows, Q_BLOCK), 1)
        eye = jnp.where(jnp.bitwise_and(r, Q_BLOCK - 1) == c, 1.0, 0.0).astype(BF16)
        for g in range(ATTN_HEADS // hp):
            qaug_ref[g, :, 0:Q_BLOCK] = eye
            qaug_ref[g, :, Q_BLOCK + ATTN_HEAD_DIM:] = jnp.zeros((rows, AUG_DEPTH - Q_BLOCK - ATTN_HEAD_DIM), BF16)

    kib = kib_ref[...]
    score = jnp.zeros((sc, Q_BLOCK), F32)
    for h in range(IDX_HEADS):
        qh = qit_ref[h * IDX_HEAD_DIM:(h + 1) * IDX_HEAD_DIM, :].astype(BF16)
        rel = jnp.maximum(_dot(kib, qh), 0.0)
        score = score + rel * wit_ref[h:h + 1, :]
    score = score * ((IDX_HEADS ** -0.5) * (IDX_HEAD_DIM ** -0.5))
    kpos2 = lax.broadcasted_iota(I32, (sc, Q_BLOCK), 0)
    qpos2 = lax.broadcasted_iota(I32, (sc, Q_BLOCK), 1) + i * Q_BLOCK
    score = jnp.where(kpos2 <= qpos2, score + 0.0, -jnp.inf)
    bits = pltpu.bitcast(score, I32)
    key2 = jnp.where(bits < 0, bits ^ 0x7FFFFFFF, bits)
    key_ref[...] = key2.reshape(nt, Q_BLOCK, Q_BLOCK)

    def value_step(t, thr):
        cand = thr + jnp.left_shift(jnp.int32(1), 31 - t)
        cnt = _count(key_ref[...] >= cand[None])
        return jnp.where(cnt >= topk, cand, thr)

    thr = lax.fori_loop(0, 32, value_step, jnp.full((1, Q_BLOCK), INT_MIN, I32))

    shape3 = (nt, Q_BLOCK, Q_BLOCK)
    kpos = lax.broadcasted_iota(I32, shape3, 0) * Q_BLOCK + lax.broadcasted_iota(I32, shape3, 1)
    qpos = lax.broadcasted_iota(I32, shape3, 2) + i * Q_BLOCK
    n_gt = _count(key_ref[...] > thr[None])
    n_ge = _count(key_ref[...] >= thr[None])
    need = topk - n_gt
    tie_ref[...] = jnp.where(key_ref[...] == thr[None], kpos, jnp.int32(NO_TIE_POS))
    excess = jnp.where(jnp.logical_and(n_ge > topk, thr != KEY_NEG_INF), 1, 0)

    def index_search():
        def index_step(t, m):
            cand = m + jnp.left_shift(jnp.int32(1), 10 - t)
            cnt = _count(tie_ref[...] < cand[None])
            return jnp.where(cnt < need, cand, m)

        return lax.fori_loop(0, 11, index_step, jnp.zeros((1, Q_BLOCK), I32))

    m = lax.cond(jnp.max(excess) > 0, index_search, lambda: jnp.full((1, Q_BLOCK), NO_TIE_POS - 1, I32))
    causal_bias = jnp.where(kpos <= qpos, 0.0, NEG_BIG)
    tie_bias = jnp.where(tie_ref[...] <= m[None], causal_bias, NEG_BIG)
    mask3 = jnp.where(key_ref[...] > thr[None], causal_bias, tie_bias)
    far3 = jnp.where(kpos < n0 * Q_BLOCK, mask3, NEG_BIG)
    mask_ref[...] = mask3.reshape(sc, Q_BLOCK).astype(BF16)
    kaug_ref[:, 0:Q_BLOCK] = far3.reshape(sc, Q_BLOCK).astype(BF16)

    q = q_ref[...] * (ATTN_HEAD_DIM ** -0.5)
    for h in range(ATTN_HEADS):
        r0 = (h % hp) * Q_BLOCK
        qaug_ref[h // hp, r0:r0 + Q_BLOCK, Q_BLOCK:Q_BLOCK + ATTN_HEAD_DIM] = (
            q[:, h * ATTN_HEAD_DIM:(h + 1) * ATTN_HEAD_DIM].astype(BF16))

    near0 = pl.multiple_of(n0 * Q_BLOCK, Q_BLOCK)
    kaug_near = jnp.concatenate(
        [mask_ref[pl.ds(near0, 2 * Q_BLOCK), :], kaug_ref[pl.ds(near0, 2 * Q_BLOCK), Q_BLOCK:]], axis=1)
    v_near = vb_ref[pl.ds(near0, 2 * Q_BLOCK), :]
    pidx = jnp.minimum(i, 1)

    def head_group(g, carry):
        qa = qaug_ref[g]
        lf = _dot_nt(qa, kaug_ref[...])
        ln = _dot_nt(qa, kaug_near) + bias_ref[pidx, g]
        mx = jnp.maximum(jnp.max(lf, axis=1, keepdims=True), jnp.max(ln, axis=1, keepdims=True))
        pf = jnp.exp((lf - mx).astype(BF16))
        pn = jnp.exp((ln - mx).astype(BF16))
        acc = _dot(pf, vb_ref[...]) + _dot(pn, v_near)
        den = acc[:, ATTN_HEAD_DIM:ATTN_HEAD_DIM + 1]
        oh_ref[g] = acc[:, 0:ATTN_HEAD_DIM] / den
        return carry

    lax.fori_loop(0, ATTN_HEADS // hp, head_group, 0, unroll=2)
    o_ref[...] = jnp.concatenate(
        [oh_ref[h // hp, (h % hp) * Q_BLOCK:(h % hp + 1) * Q_BLOCK, :] for h in range(ATTN_HEADS)],
        axis=1).astype(BF16)


def dsa_attention(zrow, idx_t, bias_tiles, *, qb0, qb1, hp=2):
    b, s, _ = zrow.shape
    sc = qb1 * Q_BLOCK
    assert sc <= 2048, "the tie-break index search covers 11 bits"
    nq = qb1 - qb0
    nqs = s // Q_BLOCK
    ng = ATTN_HEADS // hp
    rows = hp * Q_BLOCK
    topk = min(TOPK_MAX, s // TOPK_DIVISOR)
    kv_spec = lambda col: pl.BlockSpec((None, sc, 128), lambda bi, qi: (bi, 0, col // 128))
    return pl.pallas_call(
        functools.partial(_dsa_kernel, qb0=qb0, topk=topk, hp=hp),
        out_shape=jax.ShapeDtypeStruct((b, nq * Q_BLOCK, ATTN_WIDTH), BF16),
        grid=(b, nq),
        in_specs=[
            pl.BlockSpec((None, Q_BLOCK, ATTN_WIDTH), lambda bi, qi: (bi, qi + qb0, COL_Q // ATTN_WIDTH)),
            kv_spec(COL_K),
            kv_spec(COL_V),
            kv_spec(COL_KI),
            pl.BlockSpec((IDX_HEADS * IDX_HEAD_DIM, Q_BLOCK), lambda bi, qi: (0, bi * nqs + qi + qb0)),
            pl.BlockSpec((16, Q_BLOCK), lambda bi, qi: (IDX_HEADS * IDX_HEAD_DIM // 16, bi * nqs + qi + qb0)),
            pl.BlockSpec((2, ng, rows, 2 * Q_BLOCK), lambda bi, qi: (0, 0, 0, 0)),
        ],
        out_specs=pl.BlockSpec((None, Q_BLOCK, ATTN_WIDTH), lambda bi, qi: (bi, qi, 0)),
        scratch_shapes=[
            pltpu.VMEM((sc, AUG_DEPTH), BF16),
            pltpu.VMEM((sc, 128), BF16),
            pltpu.VMEM((sc, IDX_HEAD_DIM), BF16),
            pltpu.VMEM((ng, rows, AUG_DEPTH), BF16),
            pltpu.VMEM((sc // Q_BLOCK, Q_BLOCK, Q_BLOCK), I32),
            pltpu.VMEM((sc // Q_BLOCK, Q_BLOCK, Q_BLOCK), I32),
            pltpu.VMEM((sc, Q_BLOCK), BF16),
            pltpu.VMEM((ng, rows, ATTN_HEAD_DIM), F32),
        ],
        compiler_params=_params("parallel", "arbitrary"),
        name="dsa_attention",
    )(zrow, zrow, zrow, zrow, idx_t, idx_t, bias_tiles.reshape(2, ng, rows, 2 * Q_BLOCK))


def _merge_kernel(x_ref, gpre_ref, wg_ref, pool_ref, wpool_ref, attn_ref, wattn_ref, s5_ref, wglu_ref,
                  wout_ref, gpost_ref, o_ref):
    x = x_ref[...]
    h = _rms(x, gpre_ref[...]).astype(BF16)
    gates = jax.nn.sigmoid(_dot(h, wg_ref[...]))
    y_pool = _dot(pool_ref[...], wpool_ref[...])
    y_attn = _dot(attn_ref[...], wattn_ref[...])
    glu = _dot(s5_ref[...], wglu_ref[...])
    y_s5 = glu[:, :D_MODEL] * jax.nn.sigmoid(glu[:, D_MODEL:])
    merged = (gates[:, 0:D_MODEL] * y_pool + gates[:, D_MODEL:2 * D_MODEL] * y_attn
              + gates[:, 2 * D_MODEL:3 * D_MODEL] * y_s5)
    mixed = _dot(merged.astype(BF16), wout_ref[...])
    o_ref[...] = x + _rms(mixed, gpost_ref[...])


def merge(x, g_pre, w_gates, pool_y, w_pool, attn_y, w_attn, s5_y, w_glu, w_out, g_post, *, tm=256):
    t, d = x.shape
    row = lambda width: pl.BlockSpec((tm, width), lambda i: (i, 0))
    full = lambda a: _resident(a.shape)
    return pl.pallas_call(
        _merge_kernel,
        out_shape=jax.ShapeDtypeStruct((t, d), F32),
        grid=(t // tm,),
        in_specs=[row(d), full(g_pre), full(w_gates), row(POOL_WIDTH), full(w_pool), row(ATTN_WIDTH),
                  full(w_attn), row(S5_WIDTH), full(w_glu), full(w_out), full(g_post)],
        out_specs=row(d),
        compiler_params=_params("parallel"),
        name="merge",
    )(x, g_pre, w_gates, pool_y, w_pool, attn_y, w_attn, s5_y, w_glu, w_out, g_post)


def _ffn_kernel(x_ref, gpre_ref, wgate_ref, wup_ref, wdown_ref, gpost_ref, o_ref, h_ref, acc_ref):
    j = pl.program_id(1)

    @pl.when(j == 0)
    def _():
        h_ref[...] = _rms(x_ref[...], gpre_ref[...]).astype(BF16)
        acc_ref[...] = jnp.zeros_like(acc_ref)

    h = h_ref[...]
    gate = _dot(h, wgate_ref[...])
    up = _dot(h, wup_ref[...])
    act = (jax.nn.silu(gate) * up).astype(BF16)
    acc_ref[...] += _dot(act, wdown_ref[...])

    @pl.when(j == pl.num_programs(1) - 1)
    def _():
        o_ref[...] = x_ref[...] + _rms(acc_ref[...], gpost_ref[...])


def ffn(x, g_pre, w_in, w_down, g_post, *, tm=1024, th=256):
    t, d = x.shape
    hid = w_down.shape[0]
    nh = hid // th
    return pl.pallas_call(
        _ffn_kernel,
        out_shape=jax.ShapeDtypeStruct((t, d), F32),
        grid=(t // tm, nh),
        in_specs=[
            pl.BlockSpec((tm, d), lambda i, j: (i, 0)),
            pl.BlockSpec((1, d), lambda i, j: (0, 0)),
            pl.BlockSpec((d, th), lambda i, j: (0, j)),
            pl.BlockSpec((d, th), lambda i, j: (0, j + nh)),
            pl.BlockSpec((th, d), lambda i, j: (j, 0)),
            pl.BlockSpec((1, d), lambda i, j: (0, 0)),
        ],
        out_specs=pl.BlockSpec((tm, d), lambda i, j: (i, 0)),
        scratch_shapes=[pltpu.VMEM((tm, d), BF16), pltpu.VMEM((tm, d), F32)],
        compiler_params=_params("parallel", "arbitrary"),
        name="ffn",
    )(x, g_pre, w_in, w_in, w_down, g_post)


def _pad_cols(w, width):
    return jnp.pad(w, ((0, 0), (0, width - w.shape[1])))


def _split_w_in(w):
    sizes = (POOL_WIDTH, ATTN_WIDTH, ATTN_HEAD_DIM, ATTN_HEAD_DIM, IDX_HEADS * IDX_HEAD_DIM, IDX_HEAD_DIM,
             IDX_HEADS, S5_WIDTH, N_BRANCHES * D_MODEL)
    parts, start = [], 0
    for size in sizes:
        parts.append(w[:, start:start + size])
        start += size
    w_pool, w_q, w_k, w_v, w_qi, w_ki, w_wi, w_s5, w_gates = parts
    w_rows = jnp.concatenate(
        [w_pool, w_s5, w_q, _pad_cols(w_k, 128), _pad_cols(w_v, 128), _pad_cols(w_ki, ROW_WIDTH - COL_KI)], axis=1)
    w_cols = jnp.concatenate([w_qi, _pad_cols(w_wi, IDX_T_ROWS - IDX_HEADS * IDX_HEAD_DIM)], axis=1).T
    return w_rows.astype(BF16), w_cols.astype(BF16), w_gates.astype(BF16)


def _block_diag(blocks):
    g, r, c = blocks.shape
    eye = jnp.eye(g, dtype=blocks.dtype)
    return (blocks[:, :, None, :] * eye[:, None, :, None]).reshape(g * r, g * c)


def _attn_classes(nqs):
    n_classes = 8 if nqs % 8 == 0 and nqs >= 16 else 1
    step = nqs // n_classes
    return [(c * step, (c + 1) * step) for c in range(n_classes)]


def kernel(x, norm_mix_pre, norm_mix_post, norm_ffn_pre, norm_ffn_post, w_in, pool_mix_w, pool_scale, pool_out_w, attn_out_w, rel_bias, s5_lambda_re, s5_lambda_im, s5_log_dt, s5_b_re, s5_b_im, s5_c_re, s5_c_im, s5_d, s5_glu_w, w_out, ffn_w_in, ffn_w_out):
    b, s, d = x.shape
    t = b * s
    depth = w_in.shape[0]
    bias_tiles = rel_bias_tiles(rel_bias)
    xf = x.reshape(t, d)
    for l in range(depth):
        w_rows, w_cols, w_gates = _split_w_in(w_in[l])
        g_pre = norm_mix_pre[l][None, :]
        zrow, idx_t = in_proj(xf, g_pre, w_rows, w_cols)
        zrow = zrow.reshape(b, s, ROW_WIDTH)

        pool_y = pool_mixer(zrow, pool_mix_w[l].astype(BF16), pool_scale[l][None, :])

        a_re, a_im, bb_re, bb_im = s5_discretise(
            s5_lambda_re[l][:, None, :], s5_lambda_im[l][:, None, :], s5_log_dt[l][:, None, None],
            jnp.swapaxes(s5_b_re[l], 1, 2), jnp.swapaxes(s5_b_im[l], 1, 2))
        s5_y = s5_scan(
            zrow,
            _block_diag(bb_re).astype(BF16), _block_diag(bb_im).astype(BF16),
            _block_diag(jnp.swapaxes(s5_c_re[l], 1, 2)).astype(BF16),
            _block_diag(jnp.swapaxes(s5_c_im[l], 1, 2)).astype(BF16),
            a_re.reshape(1, S5_LANES), a_im.reshape(1, S5_LANES), s5_d[l][None, :])

        attn_y = jnp.concatenate(
            [dsa_attention(zrow, idx_t, bias_tiles, qb0=q0, qb1=q1)
             for q0, q1 in _attn_classes(s // Q_BLOCK)], axis=1)

        xf = merge(xf, g_pre, w_gates, pool_y.reshape(t, POOL_WIDTH), pool_out_w[l].astype(BF16),
                   attn_y.reshape(t, ATTN_WIDTH), attn_out_w[l].astype(BF16),
                   s5_y.reshape(t, S5_WIDTH), s5_glu_w[l].astype(BF16), w_out[l].astype(BF16),
                   norm_mix_post[l][None, :])
        xf = ffn(xf, norm_ffn_pre[l][None, :], ffn_w_in[l].astype(BF16), ffn_w_out[l].astype(BF16),
                 norm_ffn_post[l][None, :])
    return xf.reshape(b, s, d)
```

```python
import functools
import math

import jax
import jax.numpy as jnp
from jax import lax
from jax.experimental import pallas as pl
from jax.experimental.pallas import tpu as pltpu

F32 = jnp.float32
BF16 = jnp.bfloat16
I32 = jnp.int32

D_MODEL = 1024
POOL_WINDOWS = (2, 4, 8, 16)
POOL_WIDTH = 512
POOL_GROUP = 128
POOL_HALO = 16
ATTN_HEADS = 16
ATTN_HEAD_DIM = 64
ATTN_WIDTH = 1024
IDX_HEADS = 8
IDX_HEAD_DIM = 64
TOPK_MAX = 256
TOPK_DIVISOR = 4
Q_BLOCK = 128
REL_BUCKETS = 32
REL_MAX_DISTANCE = 128
S5_WIDTH = 512
S5_GROUP = 16
S5_GROUPS = 32
S5_STATE = 64
S5_LANES = S5_GROUPS * S5_STATE
N_BRANCHES = 3
FFN_HIDDEN = 2816
NORM_EPS = 1e-6

COL_POOL = 0
COL_S5 = 512
COL_Q = 1024
COL_K = 2048
KV_SLAB_K, KV_SLAB_V, KV_SLAB_KI = 0, 1, 2
ROW_WIDTH = COL_K + 3 * 128
IDX_T_ROWS = 528

NEG_BIG = -1e30
INT_MIN = -(2 ** 31)

VMEM_LIMIT = 56 * 1024 * 1024


def _rms(x, g):
    ms = jnp.mean(x * x, axis=-1, keepdims=True)
    return x * lax.rsqrt(ms + NORM_EPS) * g


def _dot(a, b):
    return jnp.dot(a, b, preferred_element_type=F32)


def _dot_nt(a, b):
    return lax.dot_general(a, b, (((1,), (1,)), ((), ())), preferred_element_type=F32)


def _params(*sem):
    return pltpu.CompilerParams(dimension_semantics=sem, vmem_limit_bytes=VMEM_LIMIT)


def _resident(shape):
    return pl.BlockSpec(shape, lambda *_: (0,) * len(shape), pipeline_mode=pl.Buffered(1))


def _in_proj_kernel(x_ref, g_ref, w_ref, wt_ref, rows_ref, kv_ref, cols_ref):
    h = _rms(x_ref[...], g_ref[...]).astype(BF16)
    z = _dot(h, w_ref[...])
    rows_ref[...] = z[:, :COL_K]
    kv = z[:, COL_K:]
    lane = lax.broadcasted_iota(I32, kv.shape, 1)
    kv_ref[...] = jnp.where(lane == KV_SLAB_V * 128 + ATTN_HEAD_DIM, 1.0, kv).astype(BF16)
    cols_ref[...] = _dot_nt(wt_ref[...], h)


def in_proj(x, gain, w, wt, *, tm=512):
    t, d = x.shape
    n = w.shape[1]
    c = wt.shape[0]
    return pl.pallas_call(
        _in_proj_kernel,
        out_shape=(jax.ShapeDtypeStruct((t, COL_K), F32), jax.ShapeDtypeStruct((t, n - COL_K), BF16),
                   jax.ShapeDtypeStruct((c, t), F32)),
        grid=(t // tm,),
        in_specs=[
            pl.BlockSpec((tm, d), lambda i: (i, 0)),
            _resident((1, d)),
            _resident((d, n)),
            _resident((c, d)),
        ],
        out_specs=(pl.BlockSpec((tm, COL_K), lambda i: (i, 0)), pl.BlockSpec((tm, n - COL_K), lambda i: (i, 0)),
                   pl.BlockSpec((c, tm), lambda i: (0, i))),
        compiler_params=_params("parallel"),
        name="in_proj",
    )(x, gain, w, wt)


def _pool_kernel(u_ref, mixw_ref, scale_ref, o_ref, buf_ref):
    s = u_ref.shape[0]
    buf_ref[0:POOL_HALO, :] = jnp.zeros((POOL_HALO, POOL_WIDTH), F32)
    buf_ref[POOL_HALO:POOL_HALO + s, :] = u_ref[...]
    pos = lax.broadcasted_iota(I32, (s, POOL_GROUP), 0).astype(F32)
    for gi, w in enumerate(POOL_WINDOWS):
        c0 = gi * POOL_GROUP
        tok = buf_ref[POOL_HALO:POOL_HALO + s, c0:c0 + POOL_GROUP]
        acc = tok
        for k in range(1, w):
            acc = acc + buf_ref[POOL_HALO - k:POOL_HALO - k + s, c0:c0 + POOL_GROUP]
        count = jnp.minimum(pos + 1.0, float(w))
        d = acc / count - tok
        y = _dot(d.astype(BF16), mixw_ref[gi])
        o_ref[:, c0:c0 + POOL_GROUP] = (y * scale_ref[:, c0:c0 + POOL_GROUP]).astype(BF16)


def pool_mixer(zrow, mix_w, scale):
    b, s, _ = zrow.shape
    return pl.pallas_call(
        _pool_kernel,
        out_shape=jax.ShapeDtypeStruct((b, s, POOL_WIDTH), BF16),
        grid=(b,),
        in_specs=[
            pl.BlockSpec((None, s, POOL_WIDTH), lambda i: (i, 0, COL_POOL // POOL_WIDTH)),
            pl.BlockSpec((len(POOL_WINDOWS), POOL_GROUP, POOL_GROUP), lambda i: (0, 0, 0)),
            pl.BlockSpec((1, POOL_WIDTH), lambda i: (0, 0)),
        ],
        out_specs=pl.BlockSpec((None, s, POOL_WIDTH), lambda i: (i, 0, 0)),
        scratch_shapes=[pltpu.VMEM((POOL_HALO + s, POOL_WIDTH), F32)],
        compiler_params=_params("parallel"),
        name="pool_mixer",
    )(zrow, mix_w, scale)


def _s5_disc_kernel(lr_ref, li_ref, ldt_ref, br_ref, bi_ref, ar_ref, ai_ref, bbr_ref, bbi_ref):
    lr = lr_ref[...]
    li = li_ref[...]
    dt = jnp.exp(ldt_ref[...])
    mag = jnp.exp(lr * dt)
    a_re = mag * jnp.cos(li * dt)
    a_im = mag * jnp.sin(li * dt)
    den = lr * lr + li * li
    coef_re = ((a_re - 1.0) * lr + a_im * li) / den
    coef_im = (a_im * lr - (a_re - 1.0) * li) / den
    ar_ref[...] = a_re
    ai_ref[...] = a_im
    br = br_ref[...]
    bi = bi_ref[...]
    bbr_ref[...] = coef_re * br - coef_im * bi
    bbi_ref[...] = coef_re * bi + coef_im * br


def s5_discretise(lam_re, lam_im, log_dt, b_re_t, b_im_t):
    g, _, n = lam_re.shape
    j = b_re_t.shape[1]
    return pl.pallas_call(
        _s5_disc_kernel,
        out_shape=(
            jax.ShapeDtypeStruct((g, 1, n), F32),
            jax.ShapeDtypeStruct((g, 1, n), F32),
            jax.ShapeDtypeStruct((g, j, n), F32),
            jax.ShapeDtypeStruct((g, j, n), F32),
        ),
        name="s5_discretise",
    )(lam_re, lam_im, log_dt, b_re_t, b_im_t)


def _s5_scan_kernel(u_ref, bre_ref, bim_ref, cre_ref, cim_ref, ar_ref, ai_ref, d_ref, o_ref,
                    ut_ref, xr_ref, xi_ref, sr_ref, si_ref, y_ref, *, lane_chunk):
    nb, L, _ = u_ref.shape

    @pl.when(pl.program_id(0) == 0)
    def _():
        sr_ref[...] = jnp.zeros_like(sr_ref)
        si_ref[...] = jnp.zeros_like(si_ref)

    n_slab = S5_WIDTH // 128
    for b in range(nb):
        for c in range(n_slab):
            ut_ref[c, pl.ds(b, L, stride=nb), :] = u_ref[b, :, c * 128:(c + 1) * 128]
    u_t = jnp.concatenate([ut_ref[c] for c in range(n_slab)], axis=1)
    ub = u_t.astype(BF16)
    n_blk = S5_WIDTH // 128
    lanes_blk = S5_LANES // n_blk
    for k in range(n_blk):
        rs = slice(k * 128, (k + 1) * 128)
        ls = slice(k * lanes_blk, (k + 1) * lanes_blk)
        xr_ref[:, ls] = _dot(ub[:, rs], bre_ref[rs, ls])
        xi_ref[:, ls] = _dot(ub[:, rs], bim_ref[rs, ls])

    for c in range(S5_LANES // lane_chunk):
        cs = slice(c * lane_chunk, (c + 1) * lane_chunk)
        a_re = jnp.broadcast_to(ar_ref[:, cs], (nb, lane_chunk))
        a_im = jnp.broadcast_to(ai_ref[:, cs], (nb, lane_chunk))

        def step(t, carry, cs=cs, a_re=a_re, a_im=a_im):
            s_re, s_im = carry
            r0 = pl.multiple_of(t * nb, nb)
            n_re = a_re * s_re - a_im * s_im + xr_ref[pl.ds(r0, nb), cs]
            n_im = a_re * s_im + a_im * s_re + xi_ref[pl.ds(r0, nb), cs]
            xr_ref[pl.ds(r0, nb), cs] = n_re
            xi_ref[pl.ds(r0, nb), cs] = n_im
            return n_re, n_im

        s_re, s_im = lax.fori_loop(0, L, step, (sr_ref[:, cs], si_ref[:, cs]), unroll=2)
        sr_ref[:, cs] = s_re
        si_ref[:, cs] = s_im

    for k in range(n_blk):
        rs = slice(k * 128, (k + 1) * 128)
        ls = slice(k * lanes_blk, (k + 1) * lanes_blk)
        y = _dot(xr_ref[:, ls].astype(BF16), cre_ref[ls, rs]) - _dot(xi_ref[:, ls].astype(BF16), cim_ref[ls, rs])
        y_ref[k] = jax.nn.gelu(y + d_ref[:, rs] * u_t[:, rs])
    for b in range(nb):
        for c in range(n_slab):
            o_ref[b, :, c * 128:(c + 1) * 128] = y_ref[c, pl.ds(b, L, stride=nb), :].astype(BF16)


def s5_scan(zrow, b_re_full, b_im_full, c_re_full, c_im_full, a_re, a_im, d_skip, *, chunk=64, lane_chunk=1024):
    b, s, _ = zrow.shape
    rows = chunk * b
    return pl.pallas_call(
        functools.partial(_s5_scan_kernel, lane_chunk=lane_chunk),
        out_shape=jax.ShapeDtypeStruct((b, s, S5_WIDTH), BF16),
        grid=(s // chunk,),
        in_specs=[
            pl.BlockSpec((b, chunk, S5_WIDTH), lambda i: (0, i, COL_S5 // S5_WIDTH)),
            _resident((S5_WIDTH, S5_LANES)),
            _resident((S5_WIDTH, S5_LANES)),
            _resident((S5_LANES, S5_WIDTH)),
            _resident((S5_LANES, S5_WIDTH)),
            _resident((1, S5_LANES)),
            _resident((1, S5_LANES)),
            _resident((1, S5_WIDTH)),
        ],
        out_specs=pl.BlockSpec((b, chunk, S5_WIDTH), lambda i: (0, i, 0)),
        scratch_shapes=[
            pltpu.VMEM((S5_WIDTH // 128, rows, 128), F32),
            pltpu.VMEM((rows, S5_LANES), F32),
            pltpu.VMEM((rows, S5_LANES), F32),
            pltpu.VMEM((b, S5_LANES), F32),
            pltpu.VMEM((b, S5_LANES), F32),
            pltpu.VMEM((S5_WIDTH // 128, rows, 128), F32),
        ],
        compiler_params=_params("arbitrary"),
        name="s5_scan",
    )(zrow, b_re_full, b_im_full, c_re_full, c_im_full, a_re, a_im, d_skip)


def _rel_bias_kernel(rb_ref, o_ref):
    p = pl.program_id(0)
    h = pl.program_id(1)
    ql = lax.broadcasted_iota(I32, (Q_BLOCK, 2 * Q_BLOCK), 0)
    c = lax.broadcasted_iota(I32, (Q_BLOCK, 2 * Q_BLOCK), 1)
    dist = jnp.maximum(ql - c + p * Q_BLOCK, 0)
    max_exact = REL_BUCKETS // 2
    d_f = jnp.maximum(dist, 1).astype(F32)
    large = max_exact + (jnp.log(d_f / max_exact) / math.log(REL_MAX_DISTANCE / max_exact)
                         * (REL_BUCKETS - max_exact)).astype(I32)
    large = jnp.minimum(large, REL_BUCKETS - 1)
    bucket = jnp.where(dist < max_exact, dist, large)
    acc = jnp.zeros((Q_BLOCK, 2 * Q_BLOCK), F32)
    for k in range(REL_BUCKETS):
        acc = jnp.where(bucket == k, rb_ref[k, h], acc)
    o_ref[...] = acc - rb_ref[REL_BUCKETS - 1, h]


def rel_bias_tiles(rel_bias):
    return pl.pallas_call(
        _rel_bias_kernel,
        out_shape=jax.ShapeDtypeStruct((2, ATTN_HEADS, Q_BLOCK, 2 * Q_BLOCK), F32),
        grid=(2, ATTN_HEADS),
        in_specs=[pl.BlockSpec(memory_space=pltpu.SMEM)],
        out_specs=pl.BlockSpec((None, None, Q_BLOCK, 2 * Q_BLOCK), lambda p, h: (p, h, 0, 0)),
        name="rel_bias_tiles",
    )(rel_bias)


COUNT_CHAINS = 8


def _column_sums(x, rows_per_vreg):
    nk, nq = x.shape
    n = nk // rows_per_vreg
    x = x.reshape(n, rows_per_vreg, nq)
    chains = min(COUNT_CHAINS, n)
    accs = [x[c] for c in range(chains)]
    for j in range(chains, n):
        accs[j % chains] = accs[j % chains] + x[j]
    while len(accs) > 1:
        accs = [accs[c] + accs[c + len(accs) // 2] for c in range(len(accs) // 2)]
    return jnp.sum(accs[0].astype(F32), axis=0, keepdims=True)


def _count32(mask):
    return _column_sums(jnp.where(mask, 1.0, 0.0), 8)


def _count16(mask):
    return _column_sums(jnp.where(mask, jnp.int16(1), jnp.int16(0)), 16)


KEY_NEG_INF = -2139095041
NO_TIE_POS = 2 ** 30
AUG_DEPTH = 256
I16 = jnp.int16
I16_MIN = -(2 ** 15)


def _dsa_kernel(q_ref, k_ref, v_ref, ki_ref, qit_ref, wit_ref, bias_ref, o_ref,
                kaug_ref, qaug_ref, key_ref, hi_ref, lo_ref, tie_ref, mask_ref, oh_ref, *, qb0, topk, hp):
    sc = v_ref.shape[0]
    rows = hp * Q_BLOCK
    i = pl.program_id(1) + qb0
    n0 = jnp.maximum(i - 1, 0)

    @pl.when(pl.program_id(1) == 0)
    def _():
        kaug_ref[:, Q_BLOCK:] = k_ref[...]
        r = lax.broadcasted_iota(I32, (rows, Q_BLOCK), 0)
        c = lax.broadcasted_iota(I32, (rows, Q_BLOCK), 1)
        eye = jnp.where(jnp.bitwise_and(r, Q_BLOCK - 1) == c, 1.0, 0.0).astype(BF16)
        for g in range(ATTN_HEADS // hp):
            qaug_ref[g, :, 0:Q_BLOCK] = eye
            qaug_ref[g, :, Q_BLOCK + ATTN_HEAD_DIM:] = jnp.zeros((rows, AUG_DEPTH - Q_BLOCK - ATTN_HEAD_DIM), BF16)

    kib = ki_ref[...]
    pad = jnp.zeros((128 - IDX_HEAD_DIM, Q_BLOCK), BF16)
    score = jnp.zeros((sc, Q_BLOCK), F32)
    for h in range(IDX_HEADS):
        qh = qit_ref[h * IDX_HEAD_DIM:(h + 1) * IDX_HEAD_DIM, :].astype(BF16)
        rel = jnp.maximum(_dot(kib, jnp.concatenate([qh, pad], axis=0)), 0.0)
        score = score + rel * wit_ref[h:h + 1, :]
    score = score * ((IDX_HEADS ** -0.5) * (IDX_HEAD_DIM ** -0.5))
    kpos = lax.broadcasted_iota(I32, (sc, Q_BLOCK), 0)
    qpos = lax.broadcasted_iota(I32, (sc, Q_BLOCK), 1) + i * Q_BLOCK
    score = jnp.where(kpos <= qpos, score + 0.0, -jnp.inf)
    bits = pltpu.bitcast(score, I32)
    key = jnp.where(bits < 0, bits ^ 0x7FFFFFFF, bits)
    key_ref[...] = key
    hi_ref[...] = jnp.right_shift(key, 16).astype(I16)

    def half_search(ref, k_needed):
        def step(t, thr):
            cand = thr + jnp.left_shift(jnp.int32(1), 15 - t)
            cnt = _count16(ref[...] >= cand.astype(I16))
            return jnp.where(cnt >= k_needed, cand, thr)

        return lax.fori_loop(0, 16, step, jnp.full((1, Q_BLOCK), I16_MIN, I32))

    thr_hi = half_search(hi_ref, float(topk))
    thr_hi16 = thr_hi.astype(I16)
    above = _count16(hi_ref[...] > thr_hi16)
    lo = (jnp.bitwise_and(key_ref[...], 0xFFFF) + I16_MIN).astype(I16)
    lo_ref[...] = jnp.where(hi_ref[...] == thr_hi16, lo, jnp.int16(I16_MIN))
    thr_lo = half_search(lo_ref, float(topk) - above)
    thr = thr_hi * 65536 + (thr_lo - I16_MIN)

    n_gt = _count32(key_ref[...] > thr)
    n_ge = _count32(key_ref[...] >= thr)
    need = topk - n_gt
    tie_ref[...] = jnp.where(key_ref[...] == thr, kpos, jnp.int32(NO_TIE_POS))
    at_neg_inf = thr == KEY_NEG_INF
    excess = jnp.where(jnp.logical_and(n_ge > topk, jnp.logical_not(at_neg_inf)), 1, 0)

    def index_search():
        def index_step(t, m):
            cand = m + jnp.left_shift(jnp.int32(1), 10 - t)
            cnt = _count32(tie_ref[...] < cand)
            return jnp.where(cnt < need, cand, m)

        return lax.fori_loop(0, 11, index_step, jnp.zeros((1, Q_BLOCK), I32))

    m = lax.cond(jnp.max(excess) > 0, index_search, lambda: jnp.full((1, Q_BLOCK), NO_TIE_POS - 1, I32))
    m = jnp.where(at_neg_inf, -1, m)
    mask = jnp.where(key_ref[...] > thr, 0.0, jnp.where(tie_ref[...] <= m, 0.0, NEG_BIG))
    mask_ref[...] = mask.astype(BF16)
    kaug_ref[:, 0:Q_BLOCK] = jnp.where(kpos < n0 * Q_BLOCK, mask, NEG_BIG).astype(BF16)

    q = q_ref[...] * (ATTN_HEAD_DIM ** -0.5)
    for h in range(ATTN_HEADS):
        r0 = (h % hp) * Q_BLOCK
        qaug_ref[h // hp, r0:r0 + Q_BLOCK, Q_BLOCK:Q_BLOCK + ATTN_HEAD_DIM] = (
            q[:, h * ATTN_HEAD_DIM:(h + 1) * ATTN_HEAD_DIM].astype(BF16))

    near0 = pl.multiple_of(n0 * Q_BLOCK, Q_BLOCK)
    kaug_near = jnp.concatenate(
        [mask_ref[pl.ds(near0, 2 * Q_BLOCK), :], kaug_ref[pl.ds(near0, 2 * Q_BLOCK), Q_BLOCK:]], axis=1)
    v_near = v_ref[pl.ds(near0, 2 * Q_BLOCK), :]
    pidx = jnp.minimum(i, 1)

    def head_group(g, carry):
        qa = qaug_ref[g]
        lf = _dot_nt(qa, kaug_ref[...])
        ln = _dot_nt(qa, kaug_near) + bias_ref[pidx, g]
        mx = jnp.maximum(jnp.max(lf, axis=1, keepdims=True), jnp.max(ln, axis=1, keepdims=True))
        pf = jnp.exp((lf - mx).astype(BF16))
        pn = jnp.exp((ln - mx).astype(BF16))
        acc = _dot(pf, v_ref[...]) + _dot(pn, v_near)
        den = acc[:, ATTN_HEAD_DIM:ATTN_HEAD_DIM + 1]
        oh_ref[g] = acc[:, 0:ATTN_HEAD_DIM] / den
        return carry

    lax.fori_loop(0, ATTN_HEADS // hp, head_group, 0, unroll=4)
    o_ref[...] = jnp.concatenate(
        [oh_ref[h // hp, (h % hp) * Q_BLOCK:(h % hp + 1) * Q_BLOCK, :] for h in range(ATTN_HEADS)],
        axis=1).astype(BF16)


def dsa_attention(zrow, zkv, idx_t, bias_tiles, *, qb0, qb1, hp=2):
    b, s, _ = zrow.shape
    sc = qb1 * Q_BLOCK
    assert sc <= 2048, "the tie-break index search covers 11 bits"
    nq = qb1 - qb0
    nqs = s // Q_BLOCK
    ng = ATTN_HEADS // hp
    rows = hp * Q_BLOCK
    topk = min(TOPK_MAX, s // TOPK_DIVISOR)
    kv_spec = lambda slab: pl.BlockSpec((None, sc, 128), lambda bi, qi: (bi, 0, slab))
    return pl.pallas_call(
        functools.partial(_dsa_kernel, qb0=qb0, topk=topk, hp=hp),
        out_shape=jax.ShapeDtypeStruct((b, nq * Q_BLOCK, ATTN_WIDTH), BF16),
        grid=(b, nq),
        in_specs=[
            pl.BlockSpec((None, Q_BLOCK, ATTN_WIDTH), lambda bi, qi: (bi, qi + qb0, COL_Q // ATTN_WIDTH)),
            kv_spec(KV_SLAB_K),
            kv_spec(KV_SLAB_V),
            kv_spec(KV_SLAB_KI),
            pl.BlockSpec((IDX_HEADS * IDX_HEAD_DIM, Q_BLOCK), lambda bi, qi: (0, bi * nqs + qi + qb0)),
            pl.BlockSpec((16, Q_BLOCK), lambda bi, qi: (IDX_HEADS * IDX_HEAD_DIM // 16, bi * nqs + qi + qb0)),
            _resident((2, ng, rows, 2 * Q_BLOCK)),
        ],
        out_specs=pl.BlockSpec((None, Q_BLOCK, ATTN_WIDTH), lambda bi, qi: (bi, qi, 0)),
        scratch_shapes=[
            pltpu.VMEM((sc, AUG_DEPTH), BF16),
            pltpu.VMEM((ng, rows, AUG_DEPTH), BF16),
            pltpu.VMEM((sc, Q_BLOCK), I32),
            pltpu.VMEM((sc, Q_BLOCK), I16),
            pltpu.VMEM((sc, Q_BLOCK), I16),
            pltpu.VMEM((sc, Q_BLOCK), I32),
            pltpu.VMEM((sc, Q_BLOCK), BF16),
            pltpu.VMEM((ng, rows, ATTN_HEAD_DIM), F32),
        ],
        compiler_params=_params("parallel", "arbitrary"),
        name="dsa_attention",
    )(zrow, zkv, zkv, zkv, idx_t, idx_t, bias_tiles.reshape(2, ng, rows, 2 * Q_BLOCK))


def _merge_kernel(x_ref, gpre_ref, wg_ref, pool_ref, wpool_ref, attn_ref, wattn_ref, s5_ref, wglu_ref,
                  wout_ref, gpost_ref, o_ref):
    x = x_ref[...]
    h = _rms(x, gpre_ref[...]).astype(BF16)
    gates = jax.nn.sigmoid(_dot(h, wg_ref[...]))
    y_pool = _dot(pool_ref[...], wpool_ref[...])
    y_attn = _dot(attn_ref[...], wattn_ref[...])
    glu = _dot(s5_ref[...], wglu_ref[...])
    y_s5 = glu[:, :D_MODEL] * jax.nn.sigmoid(glu[:, D_MODEL:])
    merged = (gates[:, 0:D_MODEL] * y_pool + gates[:, D_MODEL:2 * D_MODEL] * y_attn
              + gates[:, 2 * D_MODEL:3 * D_MODEL] * y_s5)
    mixed = _dot(merged.astype(BF16), wout_ref[...])
    o_ref[...] = x + _rms(mixed, gpost_ref[...])


def merge(x, g_pre, w_gates, pool_y, w_pool, attn_y, w_attn, s5_y, w_glu, w_out, g_post, *, tm=256):
    t, d = x.shape
    row = lambda width: pl.BlockSpec((tm, width), lambda i: (i, 0))
    full = lambda a: _resident(a.shape)
    return pl.pallas_call(
        _merge_kernel,
        out_shape=jax.ShapeDtypeStruct((t, d), F32),
        grid=(t // tm,),
        in_specs=[row(d), full(g_pre), full(w_gates), row(POOL_WIDTH), full(w_pool), row(ATTN_WIDTH),
                  full(w_attn), row(S5_WIDTH), full(w_glu), full(w_out), full(g_post)],
        out_specs=row(d),
        compiler_params=_params("parallel"),
        name="merge",
    )(x, g_pre, w_gates, pool_y, w_pool, attn_y, w_attn, s5_y, w_glu, w_out, g_post)


def _ffn_kernel(x_ref, gpre_ref, wgate_ref, wup_ref, wdown_ref, gpost_ref, o_ref, h_ref, acc_ref):
    j = pl.program_id(1)

    @pl.when(j == 0)
    def _():
        h_ref[...] = _rms(x_ref[...], gpre_ref[...]).astype(BF16)
        acc_ref[...] = jnp.zeros_like(acc_ref)

    h = h_ref[...]
    gate = _dot(h, wgate_ref[...])
    up = _dot(h, wup_ref[...])
    act = (jax.nn.silu(gate) * up).astype(BF16)
    acc_ref[...] += _dot(act, wdown_ref[...])

    @pl.when(j == pl.num_programs(1) - 1)
    def _():
        o_ref[...] = x_ref[...] + _rms(acc_ref[...], gpost_ref[...])


def ffn(x, g_pre, w_in, w_down, g_post, *, tm=1024, th=256):
    t, d = x.shape
    hid = w_down.shape[0]
    nh = hid // th
    return pl.pallas_call(
        _ffn_kernel,
        out_shape=jax.ShapeDtypeStruct((t, d), F32),
        grid=(t // tm, nh),
        in_specs=[
            pl.BlockSpec((tm, d), lambda i, j: (i, 0)),
            pl.BlockSpec((1, d), lambda i, j: (0, 0)),
            pl.BlockSpec((d, th), lambda i, j: (0, j)),
            pl.BlockSpec((d, th), lambda i, j: (0, j + nh)),
            pl.BlockSpec((th, d), lambda i, j: (j, 0)),
            pl.BlockSpec((1, d), lambda i, j: (0, 0)),
        ],
        out_specs=pl.BlockSpec((tm, d), lambda i, j: (i, 0)),
        scratch_shapes=[pltpu.VMEM((tm, d), BF16), pltpu.VMEM((tm, d), F32)],
        compiler_params=_params("parallel", "arbitrary"),
        name="ffn",
    )(x, g_pre, w_in, w_in, w_down, g_post)


def _pad_cols(w, width):
    return jnp.pad(w, ((0, 0), (0, width - w.shape[1])))


def _split_w_in(w):
    sizes = (POOL_WIDTH, ATTN_WIDTH, ATTN_HEAD_DIM, ATTN_HEAD_DIM, IDX_HEADS * IDX_HEAD_DIM, IDX_HEAD_DIM,
             IDX_HEADS, S5_WIDTH, N_BRANCHES * D_MODEL)
    parts, start = [], 0
    for size in sizes:
        parts.append(w[:, start:start + size])
        start += size
    w_pool, w_q, w_k, w_v, w_qi, w_ki, w_wi, w_s5, w_gates = parts
    w_rows = jnp.concatenate(
        [w_pool, w_s5, w_q, _pad_cols(w_k, 128), _pad_cols(w_v, 128), _pad_cols(w_ki, 128)], axis=1)
    w_cols = jnp.concatenate([w_qi, _pad_cols(w_wi, IDX_T_ROWS - IDX_HEADS * IDX_HEAD_DIM)], axis=1).T
    return w_rows.astype(BF16), w_cols.astype(BF16), w_gates.astype(BF16)


def _block_diag(blocks):
    g, r, c = blocks.shape
    eye = jnp.eye(g, dtype=blocks.dtype)
    return (blocks[:, :, None, :] * eye[:, None, :, None]).reshape(g * r, g * c)


def _attn_classes(nqs):
    n_classes = 8 if nqs % 8 == 0 and nqs >= 16 else 1
    step = nqs // n_classes
    return [(c * step, (c + 1) * step) for c in range(n_classes)]


def kernel(x, norm_mix_pre, norm_mix_post, norm_ffn_pre, norm_ffn_post, w_in, pool_mix_w, pool_scale, pool_out_w, attn_out_w, rel_bias, s5_lambda_re, s5_lambda_im, s5_log_dt, s5_b_re, s5_b_im, s5_c_re, s5_c_im, s5_d, s5_glu_w, w_out, ffn_w_in, ffn_w_out):
    b, s, d = x.shape
    t = b * s
    depth = w_in.shape[0]
    bias_tiles = rel_bias_tiles(rel_bias)
    xf = x.reshape(t, d)
    for l in range(depth):
        w_rows, w_cols, w_gates = _split_w_in(w_in[l])
        g_pre = norm_mix_pre[l][None, :]
        zrow, zkv, idx_t = in_proj(xf, g_pre, w_rows, w_cols)
        zrow = zrow.reshape(b, s, COL_K)
        zkv = zkv.reshape(b, s, ROW_WIDTH - COL_K)

        pool_y = pool_mixer(zrow, pool_mix_w[l].astype(BF16), pool_scale[l][None, :])

        a_re, a_im, bb_re, bb_im = s5_discretise(
            s5_lambda_re[l][:, None, :], s5_lambda_im[l][:, None, :], s5_log_dt[l][:, None, None],
            jnp.swapaxes(s5_b_re[l], 1, 2), jnp.swapaxes(s5_b_im[l], 1, 2))
        s5_y = s5_scan(
            zrow,
            _block_diag(bb_re).astype(BF16), _block_diag(bb_im).astype(BF16),
            _block_diag(jnp.swapaxes(s5_c_re[l], 1, 2)).astype(BF16),
            _block_diag(jnp.swapaxes(s5_c_im[l], 1, 2)).astype(BF16),
            a_re.reshape(1, S5_LANES), a_im.reshape(1, S5_LANES), s5_d[l][None, :])

        attn_y = jnp.concatenate(
            [dsa_attention(zrow, zkv, idx_t, bias_tiles, qb0=q0, qb1=q1)
             for q0, q1 in _attn_classes(s // Q_BLOCK)], axis=1)

        xf = merge(xf, g_pre, w_gates, pool_y.reshape(t, POOL_WIDTH), pool_out_w[l].astype(BF16),
                   attn_y.reshape(t, ATTN_WIDTH), attn_out_w[l].astype(BF16),
                   s5_y.reshape(t, S5_WIDTH), s5_glu_w[l].astype(BF16), w_out[l].astype(BF16),
                   norm_mix_post[l][None, :])
        xf = ffn(xf, norm_ffn_pre[l][None, :], ffn_w_in[l].astype(BF16), ffn_w_out[l].astype(BF16),
                 norm_ffn_post[l][None, :])
    return xf.reshape(b, s, d)
```

```python
import functools
import math

import jax
import jax.numpy as jnp
from jax import lax
from jax.experimental import pallas as pl
from jax.experimental.pallas import tpu as pltpu

F32 = jnp.float32
BF16 = jnp.bfloat16
I32 = jnp.int32

D_MODEL = 1024
POOL_WINDOWS = (2, 4, 8, 16)
POOL_WIDTH = 512
POOL_GROUP = 128
POOL_HALO = 16
ATTN_HEADS = 16
ATTN_HEAD_DIM = 64
ATTN_WIDTH = 1024
IDX_HEADS = 8
IDX_HEAD_DIM = 64
TOPK_MAX = 256
TOPK_DIVISOR = 4
Q_BLOCK = 128
REL_BUCKETS = 32
REL_MAX_DISTANCE = 128
S5_WIDTH = 512
S5_GROUP = 16
S5_GROUPS = 32
S5_STATE = 64
S5_LANES = S5_GROUPS * S5_STATE
N_BRANCHES = 3
FFN_HIDDEN = 2816
NORM_EPS = 1e-6

COL_POOL = 0
COL_S5 = 512
COL_Q = 1024
COL_K = 2048
KV_SLAB_K, KV_SLAB_V, KV_SLAB_KI = 0, 1, 2
ROW_WIDTH = COL_K + 3 * 128
IDX_T_ROWS = 528

NEG_BIG = -1e30
INT_MIN = -(2 ** 31)

VMEM_LIMIT = 56 * 1024 * 1024


def _rms(x, g):
    ms = jnp.mean(x * x, axis=-1, keepdims=True)
    return x * lax.rsqrt(ms + NORM_EPS) * g


def _dot(a, b):
    return jnp.dot(a, b, preferred_element_type=F32)


def _dot_nt(a, b):
    return lax.dot_general(a, b, (((1,), (1,)), ((), ())), preferred_element_type=F32)


def _params(*sem):
    return pltpu.CompilerParams(dimension_semantics=sem, vmem_limit_bytes=VMEM_LIMIT)


def _resident(shape):
    return pl.BlockSpec(shape, lambda *_: (0,) * len(shape), pipeline_mode=pl.Buffered(1))


def _in_proj_kernel(x_ref, g_ref, w_ref, wt_ref, rows_ref, kv_ref, cols_ref):
    h = _rms(x_ref[...], g_ref[...]).astype(BF16)
    z = _dot(h, w_ref[...])
    rows_ref[...] = z[:, :COL_K]
    kv = z[:, COL_K:]
    lane = lax.broadcasted_iota(I32, kv.shape, 1)
    kv_ref[...] = jnp.where(lane == KV_SLAB_V * 128 + ATTN_HEAD_DIM, 1.0, kv).astype(BF16)
    cols_ref[...] = _dot_nt(wt_ref[...], h)


def in_proj(x, gain, w, wt, *, tm=512):
    t, d = x.shape
    n = w.shape[1]
    c = wt.shape[0]
    return pl.pallas_call(
        _in_proj_kernel,
        out_shape=(jax.ShapeDtypeStruct((t, COL_K), F32), jax.ShapeDtypeStruct((t, n - COL_K), BF16),
                   jax.ShapeDtypeStruct((c, t), F32)),
        grid=(t // tm,),
        in_specs=[
            pl.BlockSpec((tm, d), lambda i: (i, 0)),
            _resident((1, d)),
            _resident((d, n)),
            _resident((c, d)),
        ],
        out_specs=(pl.BlockSpec((tm, COL_K), lambda i: (i, 0)), pl.BlockSpec((tm, n - COL_K), lambda i: (i, 0)),
                   pl.BlockSpec((c, tm), lambda i: (0, i))),
        compiler_params=_params("parallel"),
        name="in_proj",
    )(x, gain, w, wt)


def _pool_kernel(u_ref, mixw_ref, scale_ref, o_ref, buf_ref):
    s = u_ref.shape[0]
    buf_ref[0:POOL_HALO, :] = jnp.zeros((POOL_HALO, POOL_WIDTH), F32)
    buf_ref[POOL_HALO:POOL_HALO + s, :] = u_ref[...]
    pos = lax.broadcasted_iota(I32, (s, POOL_GROUP), 0).astype(F32)
    for gi, w in enumerate(POOL_WINDOWS):
        c0 = gi * POOL_GROUP
        tok = buf_ref[POOL_HALO:POOL_HALO + s, c0:c0 + POOL_GROUP]
        acc = tok
        for k in range(1, w):
            acc = acc + buf_ref[POOL_HALO - k:POOL_HALO - k + s, c0:c0 + POOL_GROUP]
        count = jnp.minimum(pos + 1.0, float(w))
        d = acc / count - tok
        y = _dot(d.astype(BF16), mixw_ref[gi])
        o_ref[:, c0:c0 + POOL_GROUP] = (y * scale_ref[:, c0:c0 + POOL_GROUP]).astype(BF16)


def pool_mixer(zrow, mix_w, scale):
    b, s, _ = zrow.shape
    return pl.pallas_call(
        _pool_kernel,
        out_shape=jax.ShapeDtypeStruct((b, s, POOL_WIDTH), BF16),
        grid=(b,),
        in_specs=[
            pl.BlockSpec((None, s, POOL_WIDTH), lambda i: (i, 0, COL_POOL // POOL_WIDTH)),
            pl.BlockSpec((len(POOL_WINDOWS), POOL_GROUP, POOL_GROUP), lambda i: (0, 0, 0)),
            pl.BlockSpec((1, POOL_WIDTH), lambda i: (0, 0)),
        ],
        out_specs=pl.BlockSpec((None, s, POOL_WIDTH), lambda i: (i, 0, 0)),
        scratch_shapes=[pltpu.VMEM((POOL_HALO + s, POOL_WIDTH), F32)],
        compiler_params=_params("parallel"),
        name="pool_mixer",
    )(zrow, mix_w, scale)


def _s5_disc_kernel(lr_ref, li_ref, ldt_ref, br_ref, bi_ref, ar_ref, ai_ref, bbr_ref, bbi_ref):
    lr = lr_ref[...]
    li = li_ref[...]
    dt = jnp.exp(ldt_ref[...])
    mag = jnp.exp(lr * dt)
    a_re = mag * jnp.cos(li * dt)
    a_im = mag * jnp.sin(li * dt)
    den = lr * lr + li * li
    coef_re = ((a_re - 1.0) * lr + a_im * li) / den
    coef_im = (a_im * lr - (a_re - 1.0) * li) / den
    ar_ref[...] = a_re
    ai_ref[...] = a_im
    br = br_ref[...]
    bi = bi_ref[...]
    bbr_ref[...] = coef_re * br - coef_im * bi
    bbi_ref[...] = coef_re * bi + coef_im * br


def s5_discretise(lam_re, lam_im, log_dt, b_re_t, b_im_t):
    g, _, n = lam_re.shape
    j = b_re_t.shape[1]
    return pl.pallas_call(
        _s5_disc_kernel,
        out_shape=(
            jax.ShapeDtypeStruct((g, 1, n), F32),
            jax.ShapeDtypeStruct((g, 1, n), F32),
            jax.ShapeDtypeStruct((g, j, n), F32),
            jax.ShapeDtypeStruct((g, j, n), F32),
        ),
        name="s5_discretise",
    )(lam_re, lam_im, log_dt, b_re_t, b_im_t)


def _s5_scan_kernel(u_ref, bre_ref, bim_ref, cre_ref, cim_ref, ar_ref, ai_ref, d_ref, o_ref,
                    ut_ref, xr_ref, xi_ref, sr_ref, si_ref, y_ref, *, lane_chunk):
    nb, L, _ = u_ref.shape

    @pl.when(pl.program_id(0) == 0)
    def _():
        sr_ref[...] = jnp.zeros_like(sr_ref)
        si_ref[...] = jnp.zeros_like(si_ref)

    n_slab = S5_WIDTH // 128
    for b in range(nb):
        for c in range(n_slab):
            ut_ref[c, pl.ds(b, L, stride=nb), :] = u_ref[b, :, c * 128:(c + 1) * 128]
    u_t = jnp.concatenate([ut_ref[c] for c in range(n_slab)], axis=1)
    ub = u_t.astype(BF16)
    n_blk = S5_WIDTH // 128
    lanes_blk = S5_LANES // n_blk
    for k in range(n_blk):
        rs = slice(k * 128, (k + 1) * 128)
        ls = slice(k * lanes_blk, (k + 1) * lanes_blk)
        xr_ref[:, ls] = _dot(ub[:, rs], bre_ref[rs, ls])
        xi_ref[:, ls] = _dot(ub[:, rs], bim_ref[rs, ls])

    for c in range(S5_LANES // lane_chunk):
        cs = slice(c * lane_chunk, (c + 1) * lane_chunk)
        a_re = jnp.broadcast_to(ar_ref[:, cs], (nb, lane_chunk))
        a_im = jnp.broadcast_to(ai_ref[:, cs], (nb, lane_chunk))

        def step(t, carry, cs=cs, a_re=a_re, a_im=a_im):
            s_re, s_im = carry
            r0 = pl.multiple_of(t * nb, nb)
            n_re = a_re * s_re - a_im * s_im + xr_ref[pl.ds(r0, nb), cs]
            n_im = a_re * s_im + a_im * s_re + xi_ref[pl.ds(r0, nb), cs]
            xr_ref[pl.ds(r0, nb), cs] = n_re
            xi_ref[pl.ds(r0, nb), cs] = n_im
            return n_re, n_im

        s_re, s_im = lax.fori_loop(0, L, step, (sr_ref[:, cs], si_ref[:, cs]), unroll=2)
        sr_ref[:, cs] = s_re
        si_ref[:, cs] = s_im

    for k in range(n_blk):
        rs = slice(k * 128, (k + 1) * 128)
        ls = slice(k * lanes_blk, (k + 1) * lanes_blk)
        y = _dot(xr_ref[:, ls].astype(BF16), cre_ref[ls, rs]) - _dot(xi_ref[:, ls].astype(BF16), cim_ref[ls, rs])
        y_ref[k] = jax.nn.gelu(y + d_ref[:, rs] * u_t[:, rs])
    for b in range(nb):
        for c in range(n_slab):
            o_ref[b, :, c * 128:(c + 1) * 128] = y_ref[c, pl.ds(b, L, stride=nb), :].astype(BF16)


def s5_scan(zrow, b_re_full, b_im_full, c_re_full, c_im_full, a_re, a_im, d_skip, *, chunk=64, lane_chunk=1024):
    b, s, _ = zrow.shape
    rows = chunk * b
    return pl.pallas_call(
        functools.partial(_s5_scan_kernel, lane_chunk=lane_chunk),
        out_shape=jax.ShapeDtypeStruct((b, s, S5_WIDTH), BF16),
        grid=(s // chunk,),
        in_specs=[
            pl.BlockSpec((b, chunk, S5_WIDTH), lambda i: (0, i, COL_S5 // S5_WIDTH)),
            _resident((S5_WIDTH, S5_LANES)),
            _resident((S5_WIDTH, S5_LANES)),
            _resident((S5_LANES, S5_WIDTH)),
            _resident((S5_LANES, S5_WIDTH)),
            _resident((1, S5_LANES)),
            _resident((1, S5_LANES)),
            _resident((1, S5_WIDTH)),
        ],
        out_specs=pl.BlockSpec((b, chunk, S5_WIDTH), lambda i: (0, i, 0)),
        scratch_shapes=[
            pltpu.VMEM((S5_WIDTH // 128, rows, 128), F32),
            pltpu.VMEM((rows, S5_LANES), F32),
            pltpu.VMEM((rows, S5_LANES), F32),
            pltpu.VMEM((b, S5_LANES), F32),
            pltpu.VMEM((b, S5_LANES), F32),
            pltpu.VMEM((S5_WIDTH // 128, rows, 128), F32),
        ],
        compiler_params=_params("arbitrary"),
        name="s5_scan",
    )(zrow, b_re_full, b_im_full, c_re_full, c_im_full, a_re, a_im, d_skip)


def _rel_bias_kernel(rb_ref, o_ref):
    p = pl.program_id(0)
    h = pl.program_id(1)
    ql = lax.broadcasted_iota(I32, (Q_BLOCK, 2 * Q_BLOCK), 0)
    c = lax.broadcasted_iota(I32, (Q_BLOCK, 2 * Q_BLOCK), 1)
    dist = jnp.maximum(ql - c + p * Q_BLOCK, 0)
    max_exact = REL_BUCKETS // 2
    d_f = jnp.maximum(dist, 1).astype(F32)
    large = max_exact + (jnp.log(d_f / max_exact) / math.log(REL_MAX_DISTANCE / max_exact)
                         * (REL_BUCKETS - max_exact)).astype(I32)
    large = jnp.minimum(large, REL_BUCKETS - 1)
    bucket = jnp.where(dist < max_exact, dist, large)
    acc = jnp.zeros((Q_BLOCK, 2 * Q_BLOCK), F32)
    for k in range(REL_BUCKETS):
        acc = jnp.where(bucket == k, rb_ref[k, h], acc)
    o_ref[...] = acc - rb_ref[REL_BUCKETS - 1, h]


def rel_bias_tiles(rel_bias):
    return pl.pallas_call(
        _rel_bias_kernel,
        out_shape=jax.ShapeDtypeStruct((2, ATTN_HEADS, Q_BLOCK, 2 * Q_BLOCK), F32),
        grid=(2, ATTN_HEADS),
        in_specs=[pl.BlockSpec(memory_space=pltpu.SMEM)],
        out_specs=pl.BlockSpec((None, None, Q_BLOCK, 2 * Q_BLOCK), lambda p, h: (p, h, 0, 0)),
        name="rel_bias_tiles",
    )(rel_bias)


COUNT_CHAINS = 8


def _column_sums(x, rows_per_vreg):
    nk, nq = x.shape
    n = nk // rows_per_vreg
    x = x.reshape(n, rows_per_vreg, nq)
    chains = min(COUNT_CHAINS, n)
    accs = [x[c] for c in range(chains)]
    for j in range(chains, n):
        accs[j % chains] = accs[j % chains] + x[j]
    while len(accs) > 1:
        accs = [accs[c] + accs[c + len(accs) // 2] for c in range(len(accs) // 2)]
    return jnp.sum(accs[0].astype(F32), axis=0, keepdims=True)


def _count32(mask):
    return _column_sums(jnp.where(mask, 1.0, 0.0), 8)


def _count16(mask):
    return _column_sums(jnp.where(mask, jnp.int16(1), jnp.int16(0)), 16)


KEY_NEG_INF = -2139095041
NO_TIE_POS = 2 ** 30
AUG_DEPTH = 256
KEY_CHUNK = 512
I16 = jnp.int16
I16_MIN = -(2 ** 15)


def _dsa_kernel(q_ref, k_ref, v_ref, ki_ref, qit_ref, wit_ref, bias_ref, o_ref,
                kaug_ref, qaug_ref, key_ref, hi_ref, lo_ref, tie_ref, mask_ref, oh_ref, logit_ref, *, qb0, topk, hp):
    sc = v_ref.shape[0]
    rows = hp * Q_BLOCK
    i = pl.program_id(1) + qb0
    n0 = jnp.maximum(i - 1, 0)

    @pl.when(pl.program_id(1) == 0)
    def _():
        kaug_ref[:, Q_BLOCK:] = k_ref[...]
        r = lax.broadcasted_iota(I32, (rows, Q_BLOCK), 0)
        c = lax.broadcasted_iota(I32, (rows, Q_BLOCK), 1)
        eye = jnp.where(jnp.bitwise_and(r, Q_BLOCK - 1) == c, 1.0, 0.0).astype(BF16)
        for g in range(ATTN_HEADS // hp):
            qaug_ref[g, :, 0:Q_BLOCK] = eye
            qaug_ref[g, :, Q_BLOCK + ATTN_HEAD_DIM:] = jnp.zeros((rows, AUG_DEPTH - Q_BLOCK - ATTN_HEAD_DIM), BF16)

    kib = ki_ref[...]
    pad = jnp.zeros((128 - IDX_HEAD_DIM, 2 * Q_BLOCK), BF16)
    score = jnp.zeros((sc, Q_BLOCK), F32)
    for h in range(0, IDX_HEADS, 2):
        qh = jnp.concatenate(
            [qit_ref[h * IDX_HEAD_DIM:(h + 1) * IDX_HEAD_DIM, :], qit_ref[(h + 1) * IDX_HEAD_DIM:(h + 2) * IDX_HEAD_DIM, :]],
            axis=1).astype(BF16)
        rel = jnp.maximum(_dot(kib, jnp.concatenate([qh, pad], axis=0)), 0.0)
        score = score + rel[:, :Q_BLOCK] * wit_ref[h:h + 1, :] + rel[:, Q_BLOCK:] * wit_ref[h + 1:h + 2, :]
    score = score * ((IDX_HEADS ** -0.5) * (IDX_HEAD_DIM ** -0.5))
    kpos = lax.broadcasted_iota(I32, (sc, Q_BLOCK), 0)
    qpos = lax.broadcasted_iota(I32, (sc, Q_BLOCK), 1) + i * Q_BLOCK
    score = jnp.where(kpos <= qpos, score + 0.0, -jnp.inf)
    bits = pltpu.bitcast(score, I32)
    key = jnp.where(bits < 0, bits ^ 0x7FFFFFFF, bits)
    key_ref[...] = key
    hi_ref[...] = jnp.right_shift(key, 16).astype(I16)

    def half_search(ref, k_needed):
        def step(t, thr):
            cand = thr + jnp.left_shift(jnp.int32(1), 15 - t)
            cnt = _count16(ref[...] >= cand.astype(I16))
            return jnp.where(cnt >= k_needed, cand, thr)

        return lax.fori_loop(0, 16, step, jnp.full((1, Q_BLOCK), I16_MIN, I32))

    thr_hi = half_search(hi_ref, float(topk))
    thr_hi16 = thr_hi.astype(I16)
    above = _count16(hi_ref[...] > thr_hi16)
    lo = (jnp.bitwise_and(key_ref[...], 0xFFFF) + I16_MIN).astype(I16)
    lo_ref[...] = jnp.where(hi_ref[...] == thr_hi16, lo, jnp.int16(I16_MIN))
    thr_lo = half_search(lo_ref, float(topk) - above)
    thr = thr_hi * 65536 + (thr_lo - I16_MIN)

    n_gt = _count32(key_ref[...] > thr)
    n_ge = _count32(key_ref[...] >= thr)
    need = topk - n_gt
    tie_ref[...] = jnp.where(key_ref[...] == thr, kpos, jnp.int32(NO_TIE_POS))
    at_neg_inf = thr == KEY_NEG_INF
    excess = jnp.where(jnp.logical_and(n_ge > topk, jnp.logical_not(at_neg_inf)), 1, 0)

    def index_search():
        def index_step(t, m):
            cand = m + jnp.left_shift(jnp.int32(1), 10 - t)
            cnt = _count32(tie_ref[...] < cand)
            return jnp.where(cnt < need, cand, m)

        return lax.fori_loop(0, 11, index_step, jnp.zeros((1, Q_BLOCK), I32))

    m = lax.cond(jnp.max(excess) > 0, index_search, lambda: jnp.full((1, Q_BLOCK), NO_TIE_POS - 1, I32))
    m = jnp.where(at_neg_inf, -1, m)
    mask = jnp.where(key_ref[...] > thr, 0.0, jnp.where(tie_ref[...] <= m, 0.0, NEG_BIG))
    mask_ref[...] = mask.astype(BF16)
    kaug_ref[:, 0:Q_BLOCK] = jnp.where(kpos < n0 * Q_BLOCK, mask, NEG_BIG).astype(BF16)

    q = q_ref[...] * (ATTN_HEAD_DIM ** -0.5)
    for h in range(ATTN_HEADS):
        r0 = (h % hp) * Q_BLOCK
        qaug_ref[h // hp, r0:r0 + Q_BLOCK, Q_BLOCK:Q_BLOCK + ATTN_HEAD_DIM] = (
            q[:, h * ATTN_HEAD_DIM:(h + 1) * ATTN_HEAD_DIM].astype(BF16))

    near0 = pl.multiple_of(n0 * Q_BLOCK, Q_BLOCK)
    kaug_near = jnp.concatenate(
        [mask_ref[pl.ds(near0, 2 * Q_BLOCK), :], kaug_ref[pl.ds(near0, 2 * Q_BLOCK), Q_BLOCK:]], axis=1)
    v_near = v_ref[pl.ds(near0, 2 * Q_BLOCK), :]
    pidx = jnp.minimum(i, 1)

    chunks = [(c0, min(c0 + KEY_CHUNK, sc)) for c0 in range(0, sc, KEY_CHUNK)] + [None]
    n_groups = ATTN_HEADS // hp

    def logits_chunk(g, chunk):
        slot = g % 2
        if chunk is None:
            l = _dot_nt(qaug_ref[g], kaug_near) + bias_ref[pidx, g]
            logit_ref[slot, :, sc:] = l
        else:
            l = _dot_nt(qaug_ref[g], kaug_ref[chunk[0]:chunk[1], :])
            logit_ref[slot, :, chunk[0]:chunk[1]] = l
        return jnp.max(l, axis=1, keepdims=True)

    def values_chunk(g, chunk, mx):
        slot = g % 2
        if chunk is None:
            p = jnp.exp((logit_ref[slot, :, sc:] - mx).astype(BF16))
            return _dot(p, v_near)
        p = jnp.exp((logit_ref[slot, :, chunk[0]:chunk[1]] - mx).astype(BF16))
        return _dot(p, v_ref[chunk[0]:chunk[1], :])

    def row_max(parts):
        while len(parts) > 1:
            parts = [jnp.maximum(parts[k], parts[k + 1]) if k + 1 < len(parts) else parts[k]
                     for k in range(0, len(parts), 2)]
        return parts[0]

    mx = row_max([logits_chunk(0, chunk) for chunk in chunks])
    for g in range(n_groups):
        acc = None
        next_parts = []
        for chunk in chunks:
            if g + 1 < n_groups:
                next_parts.append(logits_chunk(g + 1, chunk))
            part = values_chunk(g, chunk, mx)
            acc = part if acc is None else acc + part
        den = acc[:, ATTN_HEAD_DIM:ATTN_HEAD_DIM + 1]
        oh_ref[g] = acc[:, 0:ATTN_HEAD_DIM] / den
        if g + 1 < n_groups:
            mx = row_max(next_parts)
    o_ref[...] = jnp.concatenate(
        [oh_ref[h // hp, (h % hp) * Q_BLOCK:(h % hp + 1) * Q_BLOCK, :] for h in range(ATTN_HEADS)],
        axis=1).astype(BF16)


def dsa_attention(zrow, zkv, idx_t, bias_tiles, *, qb0, qb1, hp=2):
    b, s, _ = zrow.shape
    sc = qb1 * Q_BLOCK
    assert sc <= 2048, "the tie-break index search covers 11 bits"
    nq = qb1 - qb0
    nqs = s // Q_BLOCK
    ng = ATTN_HEADS // hp
    rows = hp * Q_BLOCK
    topk = min(TOPK_MAX, s // TOPK_DIVISOR)
    kv_spec = lambda slab: pl.BlockSpec((None, sc, 128), lambda bi, qi: (bi, 0, slab))
    return pl.pallas_call(
        functools.partial(_dsa_kernel, qb0=qb0, topk=topk, hp=hp),
        out_shape=jax.ShapeDtypeStruct((b, nq * Q_BLOCK, ATTN_WIDTH), BF16),
        grid=(b, nq),
        in_specs=[
            pl.BlockSpec((None, Q_BLOCK, ATTN_WIDTH), lambda bi, qi: (bi, qi + qb0, COL_Q // ATTN_WIDTH)),
            kv_spec(KV_SLAB_K),
            kv_spec(KV_SLAB_V),
            kv_spec(KV_SLAB_KI),
            pl.BlockSpec((IDX_HEADS * IDX_HEAD_DIM, Q_BLOCK), lambda bi, qi: (0, bi * nqs + qi + qb0)),
            pl.BlockSpec((16, Q_BLOCK), lambda bi, qi: (IDX_HEADS * IDX_HEAD_DIM // 16, bi * nqs + qi + qb0)),
            _resident((2, ng, rows, 2 * Q_BLOCK)),
        ],
        out_specs=pl.BlockSpec((None, Q_BLOCK, ATTN_WIDTH), lambda bi, qi: (bi, qi, 0)),
        scratch_shapes=[
            pltpu.VMEM((sc, AUG_DEPTH), BF16),
            pltpu.VMEM((ng, rows, AUG_DEPTH), BF16),
            pltpu.VMEM((sc, Q_BLOCK), I32),
            pltpu.VMEM((sc, Q_BLOCK), I16),
            pltpu.VMEM((sc, Q_BLOCK), I16),
            pltpu.VMEM((sc, Q_BLOCK), I32),
            pltpu.VMEM((sc, Q_BLOCK), BF16),
            pltpu.VMEM((ng, rows, ATTN_HEAD_DIM), F32),
            pltpu.VMEM((2, rows, sc + 2 * Q_BLOCK), F32),
        ],
        compiler_params=_params("parallel", "arbitrary"),
        name="dsa_attention",
    )(zrow, zkv, zkv, zkv, idx_t, idx_t, bias_tiles.reshape(2, ng, rows, 2 * Q_BLOCK))


def _merge_kernel(x_ref, gpre_ref, wg_ref, pool_ref, wpool_ref, attn_ref, wattn_ref, s5_ref, wglu_ref,
                  wout_ref, gpost_ref, o_ref):
    x = x_ref[...]
    h = _rms(x, gpre_ref[...]).astype(BF16)
    gates = jax.nn.sigmoid(_dot(h, wg_ref[...]))
    y_pool = _dot(pool_ref[...], wpool_ref[...])
    y_attn = _dot(attn_ref[...], wattn_ref[...])
    glu = _dot(s5_ref[...], wglu_ref[...])
    y_s5 = glu[:, :D_MODEL] * jax.nn.sigmoid(glu[:, D_MODEL:])
    merged = (gates[:, 0:D_MODEL] * y_pool + gates[:, D_MODEL:2 * D_MODEL] * y_attn
              + gates[:, 2 * D_MODEL:3 * D_MODEL] * y_s5)
    mixed = _dot(merged.astype(BF16), wout_ref[...])
    o_ref[...] = x + _rms(mixed, gpost_ref[...])


def merge(x, g_pre, w_gates, pool_y, w_pool, attn_y, w_attn, s5_y, w_glu, w_out, g_post, *, tm=256):
    t, d = x.shape
    row = lambda width: pl.BlockSpec((tm, width), lambda i: (i, 0))
    full = lambda a: _resident(a.shape)
    return pl.pallas_call(
        _merge_kernel,
        out_shape=jax.ShapeDtypeStruct((t, d), F32),
        grid=(t // tm,),
        in_specs=[row(d), full(g_pre), full(w_gates), row(POOL_WIDTH), full(w_pool), row(ATTN_WIDTH),
                  full(w_attn), row(S5_WIDTH), full(w_glu), full(w_out), full(g_post)],
        out_specs=row(d),
        compiler_params=_params("parallel"),
        name="merge",
    )(x, g_pre, w_gates, pool_y, w_pool, attn_y, w_attn, s5_y, w_glu, w_out, g_post)


def _ffn_kernel(x_ref, gpre_ref, wgate_ref, wup_ref, wdown_ref, gpost_ref, o_ref, h_ref, acc_ref):
    j = pl.program_id(1)

    @pl.when(j == 0)
    def _():
        h_ref[...] = _rms(x_ref[...], gpre_ref[...]).astype(BF16)
        acc_ref[...] = jnp.zeros_like(acc_ref)

    h = h_ref[...]
    gate = _dot(h, wgate_ref[...])
    up = _dot(h, wup_ref[...])
    act = (jax.nn.silu(gate) * up).astype(BF16)
    acc_ref[...] += _dot(act, wdown_ref[...])

    @pl.when(j == pl.num_programs(1) - 1)
    def _():
        o_ref[...] = x_ref[...] + _rms(acc_ref[...], gpost_ref[...])


def ffn(x, g_pre, w_in, w_down, g_post, *, tm=1024, th=256):
    t, d = x.shape
    hid = w_down.shape[0]
    nh = hid // th
    return pl.pallas_call(
        _ffn_kernel,
        out_shape=jax.ShapeDtypeStruct((t, d), F32),
        grid=(t // tm, nh),
        in_specs=[
            pl.BlockSpec((tm, d), lambda i, j: (i, 0)),
            pl.BlockSpec((1, d), lambda i, j: (0, 0)),
            pl.BlockSpec((d, th), lambda i, j: (0, j)),
            pl.BlockSpec((d, th), lambda i, j: (0, j + nh)),
            pl.BlockSpec((th, d), lambda i, j: (j, 0)),
            pl.BlockSpec((1, d), lambda i, j: (0, 0)),
        ],
        out_specs=pl.BlockSpec((tm, d), lambda i, j: (i, 0)),
        scratch_shapes=[pltpu.VMEM((tm, d), BF16), pltpu.VMEM((tm, d), F32)],
        compiler_params=_params("parallel", "arbitrary"),
        name="ffn",
    )(x, g_pre, w_in, w_in, w_down, g_post)


def _pad_cols(w, width):
    return jnp.pad(w, ((0, 0), (0, width - w.shape[1])))


def _split_w_in(w):
    sizes = (POOL_WIDTH, ATTN_WIDTH, ATTN_HEAD_DIM, ATTN_HEAD_DIM, IDX_HEADS * IDX_HEAD_DIM, IDX_HEAD_DIM,
             IDX_HEADS, S5_WIDTH, N_BRANCHES * D_MODEL)
    parts, start = [], 0
    for size in sizes:
        parts.append(w[:, start:start + size])
        start += size
    w_pool, w_q, w_k, w_v, w_qi, w_ki, w_wi, w_s5, w_gates = parts
    w_rows = jnp.concatenate(
        [w_pool, w_s5, w_q, _pad_cols(w_k, 128), _pad_cols(w_v, 128), _pad_cols(w_ki, 128)], axis=1)
    w_cols = jnp.concatenate([w_qi, _pad_cols(w_wi, IDX_T_ROWS - IDX_HEADS * IDX_HEAD_DIM)], axis=1).T
    return w_rows.astype(BF16), w_cols.astype(BF16), w_gates.astype(BF16)


def _block_diag(blocks):
    g, r, c = blocks.shape
    eye = jnp.eye(g, dtype=blocks.dtype)
    return (blocks[:, :, None, :] * eye[:, None, :, None]).reshape(g * r, g * c)


def _attn_classes(nqs):
    n_classes = 8 if nqs % 8 == 0 and nqs >= 16 else 1
    step = nqs // n_classes
    return [(c * step, (c + 1) * step) for c in range(n_classes)]


def kernel(x, norm_mix_pre, norm_mix_post, norm_ffn_pre, norm_ffn_post, w_in, pool_mix_w, pool_scale, pool_out_w, attn_out_w, rel_bias, s5_lambda_re, s5_lambda_im, s5_log_dt, s5_b_re, s5_b_im, s5_c_re, s5_c_im, s5_d, s5_glu_w, w_out, ffn_w_in, ffn_w_out):
    b, s, d = x.shape
    t = b * s
    depth = w_in.shape[0]
    bias_tiles = rel_bias_tiles(rel_bias)
    xf = x.reshape(t, d)
    for l in range(depth):
        w_rows, w_cols, w_gates = _split_w_in(w_in[l])
        g_pre = norm_mix_pre[l][None, :]
        zrow, zkv, idx_t = in_proj(xf, g_pre, w_rows, w_cols)
        zrow = zrow.reshape(b, s, COL_K)
        zkv = zkv.reshape(b, s, ROW_WIDTH - COL_K)

        pool_y = pool_mixer(zrow, pool_mix_w[l].astype(BF16), pool_scale[l][None, :])

        a_re, a_im, bb_re, bb_im = s5_discretise(
            s5_lambda_re[l][:, None, :], s5_lambda_im[l][:, None, :], s5_log_dt[l][:, None, None],
            jnp.swapaxes(s5_b_re[l], 1, 2), jnp.swapaxes(s5_b_im[l], 1, 2))
        s5_y = s5_scan(
            zrow,
            _block_diag(bb_re).astype(BF16), _block_diag(bb_im).astype(BF16),
            _block_diag(jnp.swapaxes(s5_c_re[l], 1, 2)).astype(BF16),
            _block_diag(jnp.swapaxes(s5_c_im[l], 1, 2)).astype(BF16),
            a_re.reshape(1, S5_LANES), a_im.reshape(1, S5_LANES), s5_d[l][None, :])

        attn_y = jnp.concatenate(
            [dsa_attention(zrow, zkv, idx_t, bias_tiles, qb0=q0, qb1=q1)
             for q0, q1 in _attn_classes(s // Q_BLOCK)], axis=1)

        xf = merge(xf, g_pre, w_gates, pool_y.reshape(t, POOL_WIDTH), pool_out_w[l].astype(BF16),
                   attn_y.reshape(t, ATTN_WIDTH), attn_out_w[l].astype(BF16),
                   s5_y.reshape(t, S5_WIDTH), s5_glu_w[l].astype(BF16), w_out[l].astype(BF16),
                   norm_mix_post[l][None, :])
        xf = ffn(xf, norm_ffn_pre[l][None, :], ffn_w_in[l].astype(BF16), ffn_w_out[l].astype(BF16),
                 norm_ffn_post[l][None, :])
    return xf.reshape(b, s, d)
```

```python
import functools
import math

import jax
import jax.numpy as jnp
from jax import lax
from jax.experimental import pallas as pl
from jax.experimental.pallas import tpu as pltpu

F32 = jnp.float32
BF16 = jnp.bfloat16
I32 = jnp.int32

D_MODEL = 1024
POOL_WINDOWS = (2, 4, 8, 16)
POOL_WIDTH = 512
POOL_GROUP = 128
POOL_HALO = 16
ATTN_HEADS = 16
ATTN_HEAD_DIM = 64
ATTN_WIDTH = 1024
IDX_HEADS = 8
IDX_HEAD_DIM = 64
TOPK_MAX = 256
TOPK_DIVISOR = 4
Q_BLOCK = 128
REL_BUCKETS = 32
REL_MAX_DISTANCE = 128
S5_WIDTH = 512
S5_GROUP = 16
S5_GROUPS = 32
S5_STATE = 64
S5_LANES = S5_GROUPS * S5_STATE
N_BRANCHES = 3
FFN_HIDDEN = 2816
NORM_EPS = 1e-6

COL_POOL = 0
COL_S5 = 512
COL_Q = 1024
COL_K = 2048
KV_SLAB_K, KV_SLAB_V, KV_SLAB_KI = 0, 1, 2
ROW_WIDTH = COL_K + 3 * 128
IDX_T_ROWS = 528

NEG_BIG = -1e30
INT_MIN = -(2 ** 31)

VMEM_LIMIT = 56 * 1024 * 1024


def _rms(x, g):
    ms = jnp.mean(x * x, axis=-1, keepdims=True)
    return x * lax.rsqrt(ms + NORM_EPS) * g


def _dot(a, b):
    return jnp.dot(a, b, preferred_element_type=F32)


def _dot_nt(a, b):
    return lax.dot_general(a, b, (((1,), (1,)), ((), ())), preferred_element_type=F32)


def _params(*sem):
    return pltpu.CompilerParams(dimension_semantics=sem, vmem_limit_bytes=VMEM_LIMIT)


def _resident(shape):
    return pl.BlockSpec(shape, lambda *_: (0,) * len(shape), pipeline_mode=pl.Buffered(1))


def _in_proj_kernel(x_ref, g_ref, w_ref, wt_ref, rows_ref, kv_ref, cols_ref):
    h = _rms(x_ref[...], g_ref[...]).astype(BF16)
    z = _dot(h, w_ref[...])
    rows_ref[...] = z[:, :COL_K]
    kv = z[:, COL_K:]
    lane = lax.broadcasted_iota(I32, kv.shape, 1)
    kv_ref[...] = jnp.where(lane == KV_SLAB_V * 128 + ATTN_HEAD_DIM, 1.0, kv).astype(BF16)
    cols_ref[...] = _dot_nt(wt_ref[...], h)


def in_proj(x, gain, w, wt, *, tm=512):
    t, d = x.shape
    n = w.shape[1]
    c = wt.shape[0]
    return pl.pallas_call(
        _in_proj_kernel,
        out_shape=(jax.ShapeDtypeStruct((t, COL_K), F32), jax.ShapeDtypeStruct((t, n - COL_K), BF16),
                   jax.ShapeDtypeStruct((c, t), F32)),
        grid=(t // tm,),
        in_specs=[
            pl.BlockSpec((tm, d), lambda i: (i, 0)),
            _resident((1, d)),
            _resident((d, n)),
            _resident((c, d)),
        ],
        out_specs=(pl.BlockSpec((tm, COL_K), lambda i: (i, 0)), pl.BlockSpec((tm, n - COL_K), lambda i: (i, 0)),
                   pl.BlockSpec((c, tm), lambda i: (0, i))),
        compiler_params=_params("parallel"),
        name="in_proj",
    )(x, gain, w, wt)


def _pool_kernel(u_ref, mixw_ref, scale_ref, o_ref, buf_ref):
    s = u_ref.shape[0]
    buf_ref[0:POOL_HALO, :] = jnp.zeros((POOL_HALO, POOL_WIDTH), F32)
    buf_ref[POOL_HALO:POOL_HALO + s, :] = u_ref[...]
    pos = lax.broadcasted_iota(I32, (s, POOL_GROUP), 0).astype(F32)
    for gi, w in enumerate(POOL_WINDOWS):
        c0 = gi * POOL_GROUP
        tok = buf_ref[POOL_HALO:POOL_HALO + s, c0:c0 + POOL_GROUP]
        acc = tok
        for k in range(1, w):
            acc = acc + buf_ref[POOL_HALO - k:POOL_HALO - k + s, c0:c0 + POOL_GROUP]
        count = jnp.minimum(pos + 1.0, float(w))
        d = acc / count - tok
        y = _dot(d.astype(BF16), mixw_ref[gi])
        o_ref[:, c0:c0 + POOL_GROUP] = (y * scale_ref[:, c0:c0 + POOL_GROUP]).astype(BF16)


def pool_mixer(zrow, mix_w, scale):
    b, s, _ = zrow.shape
    return pl.pallas_call(
        _pool_kernel,
        out_shape=jax.ShapeDtypeStruct((b, s, POOL_WIDTH), BF16),
        grid=(b,),
        in_specs=[
            pl.BlockSpec((None, s, POOL_WIDTH), lambda i: (i, 0, COL_POOL // POOL_WIDTH)),
            pl.BlockSpec((len(POOL_WINDOWS), POOL_GROUP, POOL_GROUP), lambda i: (0, 0, 0)),
            pl.BlockSpec((1, POOL_WIDTH), lambda i: (0, 0)),
        ],
        out_specs=pl.BlockSpec((None, s, POOL_WIDTH), lambda i: (i, 0, 0)),
        scratch_shapes=[pltpu.VMEM((POOL_HALO + s, POOL_WIDTH), F32)],
        compiler_params=_params("parallel"),
        name="pool_mixer",
    )(zrow, mix_w, scale)


def _s5_disc_kernel(lr_ref, li_ref, ldt_ref, br_ref, bi_ref, ar_ref, ai_ref, bbr_ref, bbi_ref):
    lr = lr_ref[...]
    li = li_ref[...]
    dt = jnp.exp(ldt_ref[...])
    mag = jnp.exp(lr * dt)
    a_re = mag * jnp.cos(li * dt)
    a_im = mag * jnp.sin(li * dt)
    den = lr * lr + li * li
    coef_re = ((a_re - 1.0) * lr + a_im * li) / den
    coef_im = (a_im * lr - (a_re - 1.0) * li) / den
    ar_ref[...] = a_re
    ai_ref[...] = a_im
    br = br_ref[...]
    bi = bi_ref[...]
    bbr_ref[...] = coef_re * br - coef_im * bi
    bbi_ref[...] = coef_re * bi + coef_im * br


def s5_discretise(lam_re, lam_im, log_dt, b_re_t, b_im_t):
    g, _, n = lam_re.shape
    j = b_re_t.shape[1]
    return pl.pallas_call(
        _s5_disc_kernel,
        out_shape=(
            jax.ShapeDtypeStruct((g, 1, n), F32),
            jax.ShapeDtypeStruct((g, 1, n), F32),
            jax.ShapeDtypeStruct((g, j, n), F32),
            jax.ShapeDtypeStruct((g, j, n), F32),
        ),
        name="s5_discretise",
    )(lam_re, lam_im, log_dt, b_re_t, b_im_t)


def _s5_scan_kernel(u_ref, bre_ref, bim_ref, cre_ref, cim_ref, ar_ref, ai_ref, d_ref, o_ref,
                    ut_ref, xr_ref, xi_ref, sr_ref, si_ref, y_ref, *, lane_chunk):
    nb, L, _ = u_ref.shape

    @pl.when(pl.program_id(0) == 0)
    def _():
        sr_ref[...] = jnp.zeros_like(sr_ref)
        si_ref[...] = jnp.zeros_like(si_ref)

    n_slab = S5_WIDTH // 128
    for b in range(nb):
        for c in range(n_slab):
            ut_ref[c, pl.ds(b, L, stride=nb), :] = u_ref[b, :, c * 128:(c + 1) * 128]
    u_t = jnp.concatenate([ut_ref[c] for c in range(n_slab)], axis=1)
    ub = u_t.astype(BF16)
    n_blk = S5_WIDTH // 128
    lanes_blk = S5_LANES // n_blk
    for k in range(n_blk):
        rs = slice(k * 128, (k + 1) * 128)
        ls = slice(k * lanes_blk, (k + 1) * lanes_blk)
        xr_ref[:, ls] = _dot(ub[:, rs], bre_ref[rs, ls])
        xi_ref[:, ls] = _dot(ub[:, rs], bim_ref[rs, ls])

    for c in range(S5_LANES // lane_chunk):
        cs = slice(c * lane_chunk, (c + 1) * lane_chunk)
        a_re = jnp.broadcast_to(ar_ref[:, cs], (nb, lane_chunk))
        a_im = jnp.broadcast_to(ai_ref[:, cs], (nb, lane_chunk))

        def step(t, carry, cs=cs, a_re=a_re, a_im=a_im):
            s_re, s_im = carry
            r0 = pl.multiple_of(t * nb, nb)
            n_re = a_re * s_re - a_im * s_im + xr_ref[pl.ds(r0, nb), cs]
            n_im = a_re * s_im + a_im * s_re + xi_ref[pl.ds(r0, nb), cs]
            xr_ref[pl.ds(r0, nb), cs] = n_re
            xi_ref[pl.ds(r0, nb), cs] = n_im
            return n_re, n_im

        s_re, s_im = lax.fori_loop(0, L, step, (sr_ref[:, cs], si_ref[:, cs]), unroll=2)
        sr_ref[:, cs] = s_re
        si_ref[:, cs] = s_im

    for k in range(n_blk):
        rs = slice(k * 128, (k + 1) * 128)
        ls = slice(k * lanes_blk, (k + 1) * lanes_blk)
        y = _dot(xr_ref[:, ls].astype(BF16), cre_ref[ls, rs]) - _dot(xi_ref[:, ls].astype(BF16), cim_ref[ls, rs])
        y_ref[k] = jax.nn.gelu(y + d_ref[:, rs] * u_t[:, rs])
    for b in range(nb):
        for c in range(n_slab):
            o_ref[b, :, c * 128:(c + 1) * 128] = y_ref[c, pl.ds(b, L, stride=nb), :].astype(BF16)


def s5_scan(zrow, b_re_full, b_im_full, c_re_full, c_im_full, a_re, a_im, d_skip, *, chunk=64, lane_chunk=1024):
    b, s, _ = zrow.shape
    rows = chunk * b
    return pl.pallas_call(
        functools.partial(_s5_scan_kernel, lane_chunk=lane_chunk),
        out_shape=jax.ShapeDtypeStruct((b, s, S5_WIDTH), BF16),
        grid=(s // chunk,),
        in_specs=[
            pl.BlockSpec((b, chunk, S5_WIDTH), lambda i: (0, i, COL_S5 // S5_WIDTH)),
            _resident((S5_WIDTH, S5_LANES)),
            _resident((S5_WIDTH, S5_LANES)),
            _resident((S5_LANES, S5_WIDTH)),
            _resident((S5_LANES, S5_WIDTH)),
            _resident((1, S5_LANES)),
            _resident((1, S5_LANES)),
            _resident((1, S5_WIDTH)),
        ],
        out_specs=pl.BlockSpec((b, chunk, S5_WIDTH), lambda i: (0, i, 0)),
        scratch_shapes=[
            pltpu.VMEM((S5_WIDTH // 128, rows, 128), F32),
            pltpu.VMEM((rows, S5_LANES), F32),
            pltpu.VMEM((rows, S5_LANES), F32),
            pltpu.VMEM((b, S5_LANES), F32),
            pltpu.VMEM((b, S5_LANES), F32),
            pltpu.VMEM((S5_WIDTH // 128, rows, 128), F32),
        ],
        compiler_params=_params("arbitrary"),
        name="s5_scan",
    )(zrow, b_re_full, b_im_full, c_re_full, c_im_full, a_re, a_im, d_skip)


def _rel_bias_kernel(rb_ref, o_ref):
    p = pl.program_id(0)
    h = pl.program_id(1)
    ql = lax.broadcasted_iota(I32, (Q_BLOCK, 2 * Q_BLOCK), 0)
    c = lax.broadcasted_iota(I32, (Q_BLOCK, 2 * Q_BLOCK), 1)
    dist = jnp.maximum(ql - c + p * Q_BLOCK, 0)
    max_exact = REL_BUCKETS // 2
    d_f = jnp.maximum(dist, 1).astype(F32)
    large = max_exact + (jnp.log(d_f / max_exact) / math.log(REL_MAX_DISTANCE / max_exact)
                         * (REL_BUCKETS - max_exact)).astype(I32)
    large = jnp.minimum(large, REL_BUCKETS - 1)
    bucket = jnp.where(dist < max_exact, dist, large)
    acc = jnp.zeros((Q_BLOCK, 2 * Q_BLOCK), F32)
    for k in range(REL_BUCKETS):
        acc = jnp.where(bucket == k, rb_ref[k, h], acc)
    o_ref[...] = acc - rb_ref[REL_BUCKETS - 1, h]


def rel_bias_tiles(rel_bias):
    return pl.pallas_call(
        _rel_bias_kernel,
        out_shape=jax.ShapeDtypeStruct((2, ATTN_HEADS, Q_BLOCK, 2 * Q_BLOCK), F32),
        grid=(2, ATTN_HEADS),
        in_specs=[pl.BlockSpec(memory_space=pltpu.SMEM)],
        out_specs=pl.BlockSpec((None, None, Q_BLOCK, 2 * Q_BLOCK), lambda p, h: (p, h, 0, 0)),
        name="rel_bias_tiles",
    )(rel_bias)


COUNT_CHAINS = 8


def _column_sums(x, rows_per_vreg):
    nk, nq = x.shape
    n = nk // rows_per_vreg
    x = x.reshape(n, rows_per_vreg, nq)
    chains = min(COUNT_CHAINS, n)
    accs = [x[c] for c in range(chains)]
    for j in range(chains, n):
        accs[j % chains] = accs[j % chains] + x[j]
    while len(accs) > 1:
        accs = [accs[c] + accs[c + len(accs) // 2] for c in range(len(accs) // 2)]
    return jnp.sum(accs[0].astype(F32), axis=0, keepdims=True)


def _count32(mask):
    return _column_sums(jnp.where(mask, 1.0, 0.0), 8)


def _count16(mask):
    return _column_sums(jnp.where(mask, jnp.int16(1), jnp.int16(0)), 16)


KEY_NEG_INF = -2139095041
NO_TIE_POS = 2 ** 30
AUG_DEPTH = 256
KEY_CHUNK = 2048
I16 = jnp.int16
I16_MIN = -(2 ** 15)


def _dsa_kernel(q_ref, k_ref, v_ref, ki_ref, qit_ref, wit_ref, bias_ref, o_ref,
                kaug_ref, qaug_ref, key_ref, hi_ref, lo_ref, tie_ref, mask_ref, oh_ref, logit_ref, *, qb0, topk, hp):
    sc = v_ref.shape[0]
    rows = hp * Q_BLOCK
    i = pl.program_id(1) + qb0
    n0 = jnp.maximum(i - 1, 0)

    @pl.when(pl.program_id(1) == 0)
    def _():
        kaug_ref[:, Q_BLOCK:] = k_ref[...]
        r = lax.broadcasted_iota(I32, (rows, Q_BLOCK), 0)
        c = lax.broadcasted_iota(I32, (rows, Q_BLOCK), 1)
        eye = jnp.where(jnp.bitwise_and(r, Q_BLOCK - 1) == c, 1.0, 0.0).astype(BF16)
        for g in range(ATTN_HEADS // hp):
            qaug_ref[g, :, 0:Q_BLOCK] = eye
            qaug_ref[g, :, Q_BLOCK + ATTN_HEAD_DIM:] = jnp.zeros((rows, AUG_DEPTH - Q_BLOCK - ATTN_HEAD_DIM), BF16)

    kib = ki_ref[...]
    pad = jnp.zeros((128 - IDX_HEAD_DIM, 2 * Q_BLOCK), BF16)
    score = jnp.zeros((sc, Q_BLOCK), F32)
    for h in range(0, IDX_HEADS, 2):
        qh = jnp.concatenate(
            [qit_ref[h * IDX_HEAD_DIM:(h + 1) * IDX_HEAD_DIM, :], qit_ref[(h + 1) * IDX_HEAD_DIM:(h + 2) * IDX_HEAD_DIM, :]],
            axis=1).astype(BF16)
        rel = jnp.maximum(_dot(kib, jnp.concatenate([qh, pad], axis=0)), 0.0)
        score = score + rel[:, :Q_BLOCK] * wit_ref[h:h + 1, :] + rel[:, Q_BLOCK:] * wit_ref[h + 1:h + 2, :]
    score = score * ((IDX_HEADS ** -0.5) * (IDX_HEAD_DIM ** -0.5))
    kpos = lax.broadcasted_iota(I32, (sc, Q_BLOCK), 0)
    qpos = lax.broadcasted_iota(I32, (sc, Q_BLOCK), 1) + i * Q_BLOCK
    score = jnp.where(kpos <= qpos, score + 0.0, -jnp.inf)
    bits = pltpu.bitcast(score, I32)
    key = jnp.where(bits < 0, bits ^ 0x7FFFFFFF, bits)
    key_ref[...] = key
    hi_ref[...] = jnp.right_shift(key, 16).astype(I16)

    def half_search(ref, k_needed):
        def step(t, thr):
            cand = thr + jnp.left_shift(jnp.int32(1), 15 - t)
            cnt = _count16(ref[...] >= cand.astype(I16))
            return jnp.where(cnt >= k_needed, cand, thr)

        return lax.fori_loop(0, 16, step, jnp.full((1, Q_BLOCK), I16_MIN, I32))

    thr_hi = half_search(hi_ref, float(topk))
    thr_hi16 = thr_hi.astype(I16)
    above = _count16(hi_ref[...] > thr_hi16)
    lo = (jnp.bitwise_and(key_ref[...], 0xFFFF) + I16_MIN).astype(I16)
    lo_ref[...] = jnp.where(hi_ref[...] == thr_hi16, lo, jnp.int16(I16_MIN))
    thr_lo = half_search(lo_ref, float(topk) - above)
    thr = thr_hi * 65536 + (thr_lo - I16_MIN)

    n_gt = _count32(key_ref[...] > thr)
    n_ge = _count32(key_ref[...] >= thr)
    need = topk - n_gt
    tie_ref[...] = jnp.where(key_ref[...] == thr, kpos, jnp.int32(NO_TIE_POS))
    at_neg_inf = thr == KEY_NEG_INF
    excess = jnp.where(jnp.logical_and(n_ge > topk, jnp.logical_not(at_neg_inf)), 1, 0)

    def index_search():
        def index_step(t, m):
            cand = m + jnp.left_shift(jnp.int32(1), 10 - t)
            cnt = _count32(tie_ref[...] < cand)
            return jnp.where(cnt < need, cand, m)

        return lax.fori_loop(0, 11, index_step, jnp.zeros((1, Q_BLOCK), I32))

    m = lax.cond(jnp.max(excess) > 0, index_search, lambda: jnp.full((1, Q_BLOCK), NO_TIE_POS - 1, I32))
    m = jnp.where(at_neg_inf, -1, m)
    mask = jnp.where(key_ref[...] > thr, 0.0, jnp.where(tie_ref[...] <= m, 0.0, NEG_BIG))
    mask_ref[...] = mask.astype(BF16)
    kaug_ref[:, 0:Q_BLOCK] = jnp.where(kpos < n0 * Q_BLOCK, mask, NEG_BIG).astype(BF16)

    q = q_ref[...] * (ATTN_HEAD_DIM ** -0.5)
    for h in range(ATTN_HEADS):
        r0 = (h % hp) * Q_BLOCK
        qaug_ref[h // hp, r0:r0 + Q_BLOCK, Q_BLOCK:Q_BLOCK + ATTN_HEAD_DIM] = (
            q[:, h * ATTN_HEAD_DIM:(h + 1) * ATTN_HEAD_DIM].astype(BF16))

    near0 = pl.multiple_of(n0 * Q_BLOCK, Q_BLOCK)
    kaug_near = jnp.concatenate(
        [mask_ref[pl.ds(near0, 2 * Q_BLOCK), :], kaug_ref[pl.ds(near0, 2 * Q_BLOCK), Q_BLOCK:]], axis=1)
    v_near = v_ref[pl.ds(near0, 2 * Q_BLOCK), :]
    pidx = jnp.minimum(i, 1)

    far = sc - 2 * Q_BLOCK
    chunks = [(c0, min(c0 + KEY_CHUNK, far)) for c0 in range(0, far, KEY_CHUNK)] + [None]
    n_groups = ATTN_HEADS // hp

    def logits_chunk(g, chunk):
        slot = g % 2
        if chunk is None:
            l = _dot_nt(qaug_ref[g], kaug_near) + bias_ref[pidx, g]
            logit_ref[slot, :, sc:] = l
        else:
            l = _dot_nt(qaug_ref[g], kaug_ref[chunk[0]:chunk[1], :])
            logit_ref[slot, :, chunk[0]:chunk[1]] = l
        return jnp.max(l, axis=1, keepdims=True)

    def values_chunk(g, chunk, mx):
        slot = g % 2
        if chunk is None:
            p = jnp.exp((logit_ref[slot, :, sc:] - mx).astype(BF16))
            return _dot(p, v_near)
        p = jnp.exp((logit_ref[slot, :, chunk[0]:chunk[1]] - mx).astype(BF16))
        return _dot(p, v_ref[chunk[0]:chunk[1], :])

    def row_max(parts):
        while len(parts) > 1:
            parts = [jnp.maximum(parts[k], parts[k + 1]) if k + 1 < len(parts) else parts[k]
                     for k in range(0, len(parts), 2)]
        return parts[0]

    mx = row_max([logits_chunk(0, chunk) for chunk in chunks])
    for g in range(n_groups):
        acc = None
        next_parts = []
        for chunk in chunks:
            if g + 1 < n_groups:
                next_parts.append(logits_chunk(g + 1, chunk))
            part = values_chunk(g, chunk, mx)
            acc = part if acc is None else acc + part
        den = acc[:, ATTN_HEAD_DIM:ATTN_HEAD_DIM + 1]
        oh_ref[g] = acc[:, 0:ATTN_HEAD_DIM] / den
        if g + 1 < n_groups:
            mx = row_max(next_parts)
    o_ref[...] = jnp.concatenate(
        [oh_ref[h // hp, (h % hp) * Q_BLOCK:(h % hp + 1) * Q_BLOCK, :] for h in range(ATTN_HEADS)],
        axis=1).astype(BF16)


def dsa_attention(zrow, zkv, idx_t, bias_tiles, *, qb0, qb1, hp=2):
    b, s, _ = zrow.shape
    sc = qb1 * Q_BLOCK
    assert sc <= 2048, "the tie-break index search covers 11 bits"
    nq = qb1 - qb0
    nqs = s // Q_BLOCK
    ng = ATTN_HEADS // hp
    rows = hp * Q_BLOCK
    topk = min(TOPK_MAX, s // TOPK_DIVISOR)
    kv_spec = lambda slab: pl.BlockSpec((None, sc, 128), lambda bi, qi: (bi, 0, slab))
    return pl.pallas_call(
        functools.partial(_dsa_kernel, qb0=qb0, topk=topk, hp=hp),
        out_shape=jax.ShapeDtypeStruct((b, nq * Q_BLOCK, ATTN_WIDTH), BF16),
        grid=(b, nq),
        in_specs=[
            pl.BlockSpec((None, Q_BLOCK, ATTN_WIDTH), lambda bi, qi: (bi, qi + qb0, COL_Q // ATTN_WIDTH)),
            kv_spec(KV_SLAB_K),
            kv_spec(KV_SLAB_V),
            kv_spec(KV_SLAB_KI),
            pl.BlockSpec((IDX_HEADS * IDX_HEAD_DIM, Q_BLOCK), lambda bi, qi: (0, bi * nqs + qi + qb0)),
            pl.BlockSpec((16, Q_BLOCK), lambda bi, qi: (IDX_HEADS * IDX_HEAD_DIM // 16, bi * nqs + qi + qb0)),
            _resident((2, ng, rows, 2 * Q_BLOCK)),
        ],
        out_specs=pl.BlockSpec((None, Q_BLOCK, ATTN_WIDTH), lambda bi, qi: (bi, qi, 0)),
        scratch_shapes=[
            pltpu.VMEM((sc, AUG_DEPTH), BF16),
            pltpu.VMEM((ng, rows, AUG_DEPTH), BF16),
            pltpu.VMEM((sc, Q_BLOCK), I32),
            pltpu.VMEM((sc, Q_BLOCK), I16),
            pltpu.VMEM((sc, Q_BLOCK), I16),
            pltpu.VMEM((sc, Q_BLOCK), I32),
            pltpu.VMEM((sc, Q_BLOCK), BF16),
            pltpu.VMEM((ng, rows, ATTN_HEAD_DIM), F32),
            pltpu.VMEM((2, rows, sc + 2 * Q_BLOCK), F32),
        ],
        compiler_params=_params("parallel", "arbitrary"),
        name="dsa_attention",
    )(zrow, zkv, zkv, zkv, idx_t, idx_t, bias_tiles.reshape(2, ng, rows, 2 * Q_BLOCK))


def _merge_kernel(x_ref, gpre_ref, wg_ref, pool_ref, wpool_ref, attn_ref, wattn_ref, s5_ref, wglu_ref,
                  wout_ref, gpost_ref, o_ref):
    x = x_ref[...]
    h = _rms(x, gpre_ref[...]).astype(BF16)
    gates = jax.nn.sigmoid(_dot(h, wg_ref[...]))
    y_pool = _dot(pool_ref[...], wpool_ref[...])
    y_attn = _dot(attn_ref[...], wattn_ref[...])
    glu = _dot(s5_ref[...], wglu_ref[...])
    y_s5 = glu[:, :D_MODEL] * jax.nn.sigmoid(glu[:, D_MODEL:])
    merged = (gates[:, 0:D_MODEL] * y_pool + gates[:, D_MODEL:2 * D_MODEL] * y_attn
              + gates[:, 2 * D_MODEL:3 * D_MODEL] * y_s5)
    mixed = _dot(merged.astype(BF16), wout_ref[...])
    o_ref[...] = x + _rms(mixed, gpost_ref[...])


def merge(x, g_pre, w_gates, pool_y, w_pool, attn_y, w_attn, s5_y, w_glu, w_out, g_post, *, tm=256):
    t, d = x.shape
    row = lambda width: pl.BlockSpec((tm, width), lambda i: (i, 0))
    full = lambda a: _resident(a.shape)
    return pl.pallas_call(
        _merge_kernel,
        out_shape=jax.ShapeDtypeStruct((t, d), F32),
        grid=(t // tm,),
        in_specs=[row(d), full(g_pre), full(w_gates), row(POOL_WIDTH), full(w_pool), row(ATTN_WIDTH),
                  full(w_attn), row(S5_WIDTH), full(w_glu), full(w_out), full(g_post)],
        out_specs=row(d),
        compiler_params=_params("parallel"),
        name="merge",
    )(x, g_pre, w_gates, pool_y, w_pool, attn_y, w_attn, s5_y, w_glu, w_out, g_post)


def _ffn_kernel(x_ref, gpre_ref, wgate_ref, wup_ref, wdown_ref, gpost_ref, o_ref, h_ref, acc_ref):
    j = pl.program_id(1)

    @pl.when(j == 0)
    def _():
        h_ref[...] = _rms(x_ref[...], gpre_ref[...]).astype(BF16)
        acc_ref[...] = jnp.zeros_like(acc_ref)

    h = h_ref[...]
    gate = _dot(h, wgate_ref[...])
    up = _dot(h, wup_ref[...])
    act = (jax.nn.silu(gate) * up).astype(BF16)
    acc_ref[...] += _dot(act, wdown_ref[...])

    @pl.when(j == pl.num_programs(1) - 1)
    def _():
        o_ref[...] = x_ref[...] + _rms(acc_ref[...], gpost_ref[...])


def ffn(x, g_pre, w_in, w_down, g_post, *, tm=1024, th=256):
    t, d = x.shape
    hid = w_down.shape[0]
    nh = hid // th
    return pl.pallas_call(
        _ffn_kernel,
        out_shape=jax.ShapeDtypeStruct((t, d), F32),
        grid=(t // tm, nh),
        in_specs=[
            pl.BlockSpec((tm, d), lambda i, j: (i, 0)),
            pl.BlockSpec((1, d), lambda i, j: (0, 0)),
            pl.BlockSpec((d, th), lambda i, j: (0, j)),
            pl.BlockSpec((d, th), lambda i, j: (0, j + nh)),
            pl.BlockSpec((th, d), lambda i, j: (j, 0)),
            pl.BlockSpec((1, d), lambda i, j: (0, 0)),
        ],
        out_specs=pl.BlockSpec((tm, d), lambda i, j: (i, 0)),
        scratch_shapes=[pltpu.VMEM((tm, d), BF16), pltpu.VMEM((tm, d), F32)],
        compiler_params=_params("parallel", "arbitrary"),
        name="ffn",
    )(x, g_pre, w_in, w_in, w_down, g_post)


def _pad_cols(w, width):
    return jnp.pad(w, ((0, 0), (0, width - w.shape[1])))


def _split_w_in(w):
    sizes = (POOL_WIDTH, ATTN_WIDTH, ATTN_HEAD_DIM, ATTN_HEAD_DIM, IDX_HEADS * IDX_HEAD_DIM, IDX_HEAD_DIM,
             IDX_HEADS, S5_WIDTH, N_BRANCHES * D_MODEL)
    parts, start = [], 0
    for size in sizes:
        parts.append(w[:, start:start + size])
        start += size
    w_pool, w_q, w_k, w_v, w_qi, w_ki, w_wi, w_s5, w_gates = parts
    w_rows = jnp.concatenate(
        [w_pool, w_s5, w_q, _pad_cols(w_k, 128), _pad_cols(w_v, 128), _pad_cols(w_ki, 128)], axis=1)
    w_cols = jnp.concatenate([w_qi, _pad_cols(w_wi, IDX_T_ROWS - IDX_HEADS * IDX_HEAD_DIM)], axis=1).T
    return w_rows.astype(BF16), w_cols.astype(BF16), w_gates.astype(BF16)


def _block_diag(blocks):
    g, r, c = blocks.shape
    eye = jnp.eye(g, dtype=blocks.dtype)
    return (blocks[:, :, None, :] * eye[:, None, :, None]).reshape(g * r, g * c)


def _attn_classes(nqs):
    n_classes = 8 if nqs % 8 == 0 and nqs >= 16 else 1
    step = nqs // n_classes
    return [(c * step, (c + 1) * step) for c in range(n_classes)]


def kernel(x, norm_mix_pre, norm_mix_post, norm_ffn_pre, norm_ffn_post, w_in, pool_mix_w, pool_scale, pool_out_w, attn_out_w, rel_bias, s5_lambda_re, s5_lambda_im, s5_log_dt, s5_b_re, s5_b_im, s5_c_re, s5_c_im, s5_d, s5_glu_w, w_out, ffn_w_in, ffn_w_out):
    b, s, d = x.shape
    t = b * s
    depth = w_in.shape[0]
    bias_tiles = rel_bias_tiles(rel_bias)
    xf = x.reshape(t, d)
    for l in range(depth):
        w_rows, w_cols, w_gates = _split_w_in(w_in[l])
        g_pre = norm_mix_pre[l][None, :]
        zrow, zkv, idx_t = in_proj(xf, g_pre, w_rows, w_cols)
        zrow = zrow.reshape(b, s, COL_K)
        zkv = zkv.reshape(b, s, ROW_WIDTH - COL_K)

        pool_y = pool_mixer(zrow, pool_mix_w[l].astype(BF16), pool_scale[l][None, :])

        a_re, a_im, bb_re, bb_im = s5_discretise(
            s5_lambda_re[l][:, None, :], s5_lambda_im[l][:, None, :], s5_log_dt[l][:, None, None],
            jnp.swapaxes(s5_b_re[l], 1, 2), jnp.swapaxes(s5_b_im[l], 1, 2))
        s5_y = s5_scan(
            zrow,
            _block_diag(bb_re).astype(BF16), _block_diag(bb_im).astype(BF16),
            _block_diag(jnp.swapaxes(s5_c_re[l], 1, 2)).astype(BF16),
            _block_diag(jnp.swapaxes(s5_c_im[l], 1, 2)).astype(BF16),
            a_re.reshape(1, S5_LANES), a_im.reshape(1, S5_LANES), s5_d[l][None, :])

        attn_y = jnp.concatenate(
            [dsa_attention(zrow, zkv, idx_t, bias_tiles, qb0=q0, qb1=q1)
             for q0, q1 in _attn_classes(s // Q_BLOCK)], axis=1)

        xf = merge(xf, g_pre, w_gates, pool_y.reshape(t, POOL_WIDTH), pool_out_w[l].astype(BF16),
                   attn_y.reshape(t, ATTN_WIDTH), attn_out_w[l].astype(BF16),
                   s5_y.reshape(t, S5_WIDTH), s5_glu_w[l].astype(BF16), w_out[l].astype(BF16),
                   norm_mix_post[l][None, :])
        xf = ffn(xf, norm_ffn_pre[l][None, :], ffn_w_in[l].astype(BF16), ffn_w_out[l].astype(BF16),
                 norm_ffn_post[l][None, :])
    return xf.reshape(b, s, d)
```

```python
import functools
import math

import jax
import jax.numpy as jnp
from jax import lax
from jax.experimental import pallas as pl
from jax.experimental.pallas import tpu as pltpu

F32 = jnp.float32
BF16 = jnp.bfloat16
I32 = jnp.int32

D_MODEL = 1024
POOL_WINDOWS = (2, 4, 8, 16)
POOL_WIDTH = 512
POOL_GROUP = 128
POOL_HALO = 16
ATTN_HEADS = 16
ATTN_HEAD_DIM = 64
ATTN_WIDTH = 1024
IDX_HEADS = 8
IDX_HEAD_DIM = 64
TOPK_MAX = 256
TOPK_DIVISOR = 4
Q_BLOCK = 128
REL_BUCKETS = 32
REL_MAX_DISTANCE = 128
S5_WIDTH = 512
S5_GROUP = 16
S5_GROUPS = 32
S5_STATE = 64
S5_LANES = S5_GROUPS * S5_STATE
N_BRANCHES = 3
FFN_HIDDEN = 2816
NORM_EPS = 1e-6

COL_POOL = 0
COL_S5 = 512
COL_Q = 1024
COL_K = 2048
KV_SLAB_K, KV_SLAB_V, KV_SLAB_KI = 0, 1, 2
ROW_WIDTH = COL_K + 3 * 128
IDX_T_ROWS = 528

NEG_BIG = -1e30
INT_MIN = -(2 ** 31)

VMEM_LIMIT = 56 * 1024 * 1024


def _rms(x, g):
    ms = jnp.mean(x * x, axis=-1, keepdims=True)
    return x * lax.rsqrt(ms + NORM_EPS) * g


def _dot(a, b):
    return jnp.dot(a, b, preferred_element_type=F32)


def _dot_nt(a, b):
    return lax.dot_general(a, b, (((1,), (1,)), ((), ())), preferred_element_type=F32)


def _params(*sem):
    return pltpu.CompilerParams(dimension_semantics=sem, vmem_limit_bytes=VMEM_LIMIT)


def _resident(shape):
    return pl.BlockSpec(shape, lambda *_: (0,) * len(shape), pipeline_mode=pl.Buffered(1))


def _in_proj_kernel(x_ref, g_ref, w_ref, wt_ref, rows_ref, kv_ref, cols_ref):
    h = _rms(x_ref[...], g_ref[...]).astype(BF16)
    z = _dot(h, w_ref[...])
    rows_ref[...] = z[:, :COL_K]
    kv = z[:, COL_K:]
    lane = lax.broadcasted_iota(I32, kv.shape, 1)
    kv_ref[...] = jnp.where(lane == KV_SLAB_V * 128 + ATTN_HEAD_DIM, 1.0, kv).astype(BF16)
    cols_ref[...] = _dot_nt(wt_ref[...], h)


def in_proj(x, gain, w, wt, *, tm=512):
    t, d = x.shape
    n = w.shape[1]
    c = wt.shape[0]
    return pl.pallas_call(
        _in_proj_kernel,
        out_shape=(jax.ShapeDtypeStruct((t, COL_K), F32), jax.ShapeDtypeStruct((t, n - COL_K), BF16),
                   jax.ShapeDtypeStruct((c, t), F32)),
        grid=(t // tm,),
        in_specs=[
            pl.BlockSpec((tm, d), lambda i: (i, 0)),
            _resident((1, d)),
            _resident((d, n)),
            _resident((c, d)),
        ],
        out_specs=(pl.BlockSpec((tm, COL_K), lambda i: (i, 0)), pl.BlockSpec((tm, n - COL_K), lambda i: (i, 0)),
                   pl.BlockSpec((c, tm), lambda i: (0, i))),
        compiler_params=_params("parallel"),
        name="in_proj",
    )(x, gain, w, wt)


def _pool_kernel(u_ref, mixw_ref, scale_ref, o_ref, buf_ref):
    s = u_ref.shape[0]
    buf_ref[0:POOL_HALO, :] = jnp.zeros((POOL_HALO, POOL_WIDTH), F32)
    buf_ref[POOL_HALO:POOL_HALO + s, :] = u_ref[...]
    pos = lax.broadcasted_iota(I32, (s, POOL_GROUP), 0).astype(F32)
    for gi, w in enumerate(POOL_WINDOWS):
        c0 = gi * POOL_GROUP
        tok = buf_ref[POOL_HALO:POOL_HALO + s, c0:c0 + POOL_GROUP]
        acc = tok
        for k in range(1, w):
            acc = acc + buf_ref[POOL_HALO - k:POOL_HALO - k + s, c0:c0 + POOL_GROUP]
        count = jnp.minimum(pos + 1.0, float(w))
        d = acc / count - tok
        y = _dot(d.astype(BF16), mixw_ref[gi])
        o_ref[:, c0:c0 + POOL_GROUP] = (y * scale_ref[:, c0:c0 + POOL_GROUP]).astype(BF16)


def pool_mixer(zrow, mix_w, scale):
    b, s, _ = zrow.shape
    return pl.pallas_call(
        _pool_kernel,
        out_shape=jax.ShapeDtypeStruct((b, s, POOL_WIDTH), BF16),
        grid=(b,),
        in_specs=[
            pl.BlockSpec((None, s, POOL_WIDTH), lambda i: (i, 0, COL_POOL // POOL_WIDTH)),
            pl.BlockSpec((len(POOL_WINDOWS), POOL_GROUP, POOL_GROUP), lambda i: (0, 0, 0)),
            pl.BlockSpec((1, POOL_WIDTH), lambda i: (0, 0)),
        ],
        out_specs=pl.BlockSpec((None, s, POOL_WIDTH), lambda i: (i, 0, 0)),
        scratch_shapes=[pltpu.VMEM((POOL_HALO + s, POOL_WIDTH), F32)],
        compiler_params=_params("parallel"),
        name="pool_mixer",
    )(zrow, mix_w, scale)


def _s5_disc_kernel(lr_ref, li_ref, ldt_ref, br_ref, bi_ref, ar_ref, ai_ref, bbr_ref, bbi_ref):
    lr = lr_ref[...]
    li = li_ref[...]
    dt = jnp.exp(ldt_ref[...])
    mag = jnp.exp(lr * dt)
    a_re = mag * jnp.cos(li * dt)
    a_im = mag * jnp.sin(li * dt)
    den = lr * lr + li * li
    coef_re = ((a_re - 1.0) * lr + a_im * li) / den
    coef_im = (a_im * lr - (a_re - 1.0) * li) / den
    ar_ref[...] = a_re
    ai_ref[...] = a_im
    br = br_ref[...]
    bi = bi_ref[...]
    bbr_ref[...] = coef_re * br - coef_im * bi
    bbi_ref[...] = coef_re * bi + coef_im * br


def s5_discretise(lam_re, lam_im, log_dt, b_re_t, b_im_t):
    g, _, n = lam_re.shape
    j = b_re_t.shape[1]
    return pl.pallas_call(
        _s5_disc_kernel,
        out_shape=(
            jax.ShapeDtypeStruct((g, 1, n), F32),
            jax.ShapeDtypeStruct((g, 1, n), F32),
            jax.ShapeDtypeStruct((g, j, n), F32),
            jax.ShapeDtypeStruct((g, j, n), F32),
        ),
        name="s5_discretise",
    )(lam_re, lam_im, log_dt, b_re_t, b_im_t)


def _s5_scan_kernel(u_ref, bre_ref, bim_ref, cre_ref, cim_ref, ar_ref, ai_ref, d_ref, o_ref,
                    ut_ref, xr_ref, xi_ref, sr_ref, si_ref, y_ref, *, lane_chunk):
    nb, L, _ = u_ref.shape

    @pl.when(pl.program_id(0) == 0)
    def _():
        sr_ref[...] = jnp.zeros_like(sr_ref)
        si_ref[...] = jnp.zeros_like(si_ref)

    n_slab = S5_WIDTH // 128
    for b in range(nb):
        for c in range(n_slab):
            ut_ref[c, pl.ds(b, L, stride=nb), :] = u_ref[b, :, c * 128:(c + 1) * 128]
    u_t = jnp.concatenate([ut_ref[c] for c in range(n_slab)], axis=1)
    ub = u_t.astype(BF16)
    n_blk = S5_WIDTH // 128
    lanes_blk = S5_LANES // n_blk
    for k in range(n_blk):
        rs = slice(k * 128, (k + 1) * 128)
        ls = slice(k * lanes_blk, (k + 1) * lanes_blk)
        xr_ref[:, ls] = _dot(ub[:, rs], bre_ref[rs, ls])
        xi_ref[:, ls] = _dot(ub[:, rs], bim_ref[rs, ls])

    for c in range(S5_LANES // lane_chunk):
        cs = slice(c * lane_chunk, (c + 1) * lane_chunk)
        a_re = jnp.broadcast_to(ar_ref[:, cs], (nb, lane_chunk))
        a_im = jnp.broadcast_to(ai_ref[:, cs], (nb, lane_chunk))

        def step(t, carry, cs=cs, a_re=a_re, a_im=a_im):
            s_re, s_im = carry
            r0 = pl.multiple_of(t * nb, nb)
            n_re = a_re * s_re - a_im * s_im + xr_ref[pl.ds(r0, nb), cs]
            n_im = a_re * s_im + a_im * s_re + xi_ref[pl.ds(r0, nb), cs]
            xr_ref[pl.ds(r0, nb), cs] = n_re
            xi_ref[pl.ds(r0, nb), cs] = n_im
            return n_re, n_im

        s_re, s_im = lax.fori_loop(0, L, step, (sr_ref[:, cs], si_ref[:, cs]), unroll=2)
        sr_ref[:, cs] = s_re
        si_ref[:, cs] = s_im

    for k in range(n_blk):
        rs = slice(k * 128, (k + 1) * 128)
        ls = slice(k * lanes_blk, (k + 1) * lanes_blk)
        y = _dot(xr_ref[:, ls].astype(BF16), cre_ref[ls, rs]) - _dot(xi_ref[:, ls].astype(BF16), cim_ref[ls, rs])
        y_ref[k] = jax.nn.gelu(y + d_ref[:, rs] * u_t[:, rs])
    for b in range(nb):
        for c in range(n_slab):
            o_ref[b, :, c * 128:(c + 1) * 128] = y_ref[c, pl.ds(b, L, stride=nb), :].astype(BF16)


def s5_scan(zrow, b_re_full, b_im_full, c_re_full, c_im_full, a_re, a_im, d_skip, *, chunk=64, lane_chunk=1024):
    b, s, _ = zrow.shape
    rows = chunk * b
    return pl.pallas_call(
        functools.partial(_s5_scan_kernel, lane_chunk=lane_chunk),
        out_shape=jax.ShapeDtypeStruct((b, s, S5_WIDTH), BF16),
        grid=(s // chunk,),
        in_specs=[
            pl.BlockSpec((b, chunk, S5_WIDTH), lambda i: (0, i, COL_S5 // S5_WIDTH)),
            _resident((S5_WIDTH, S5_LANES)),
            _resident((S5_WIDTH, S5_LANES)),
            _resident((S5_LANES, S5_WIDTH)),
            _resident((S5_LANES, S5_WIDTH)),
            _resident((1, S5_LANES)),
            _resident((1, S5_LANES)),
            _resident((1, S5_WIDTH)),
        ],
        out_specs=pl.BlockSpec((b, chunk, S5_WIDTH), lambda i: (0, i, 0)),
        scratch_shapes=[
            pltpu.VMEM((S5_WIDTH // 128, rows, 128), F32),
            pltpu.VMEM((rows, S5_LANES), F32),
            pltpu.VMEM((rows, S5_LANES), F32),
            pltpu.VMEM((b, S5_LANES), F32),
            pltpu.VMEM((b, S5_LANES), F32),
            pltpu.VMEM((S5_WIDTH // 128, rows, 128), F32),
        ],
        compiler_params=_params("arbitrary"),
        name="s5_scan",
    )(zrow, b_re_full, b_im_full, c_re_full, c_im_full, a_re, a_im, d_skip)


def _rel_bias_kernel(rb_ref, o_ref):
    p = pl.program_id(0)
    h = pl.program_id(1)
    ql = lax.broadcasted_iota(I32, (Q_BLOCK, 2 * Q_BLOCK), 0)
    c = lax.broadcasted_iota(I32, (Q_BLOCK, 2 * Q_BLOCK), 1)
    dist = jnp.maximum(ql - c + p * Q_BLOCK, 0)
    max_exact = REL_BUCKETS // 2
    d_f = jnp.maximum(dist, 1).astype(F32)
    large = max_exact + (jnp.log(d_f / max_exact) / math.log(REL_MAX_DISTANCE / max_exact)
                         * (REL_BUCKETS - max_exact)).astype(I32)
    large = jnp.minimum(large, REL_BUCKETS - 1)
    bucket = jnp.where(dist < max_exact, dist, large)
    acc = jnp.zeros((Q_BLOCK, 2 * Q_BLOCK), F32)
    for k in range(REL_BUCKETS):
        acc = jnp.where(bucket == k, rb_ref[k, h], acc)
    o_ref[...] = acc - rb_ref[REL_BUCKETS - 1, h]


def rel_bias_tiles(rel_bias):
    return pl.pallas_call(
        _rel_bias_kernel,
        out_shape=jax.ShapeDtypeStruct((2, ATTN_HEADS, Q_BLOCK, 2 * Q_BLOCK), F32),
        grid=(2, ATTN_HEADS),
        in_specs=[pl.BlockSpec(memory_space=pltpu.SMEM)],
        out_specs=pl.BlockSpec((None, None, Q_BLOCK, 2 * Q_BLOCK), lambda p, h: (p, h, 0, 0)),
        name="rel_bias_tiles",
    )(rel_bias)


COUNT_CHAINS = 8


def _count(ref, pred, *, packed):
    rows = 16 if packed else 8
    one, zero = (jnp.int16(1), jnp.int16(0)) if packed else (1.0, 0.0)
    n = ref.shape[0] // rows
    chains = min(COUNT_CHAINS, n)
    accs = []
    for j in range(n):
        hit = jnp.where(pred(ref[j * rows:(j + 1) * rows, :]), one, zero)
        if j < chains:
            accs.append(hit)
        else:
            accs[j % chains] = accs[j % chains] + hit
    while len(accs) > 1:
        accs = [accs[c] + accs[c + len(accs) // 2] for c in range(len(accs) // 2)]
    return jnp.sum(accs[0].astype(F32), axis=0, keepdims=True)


KEY_NEG_INF = -2139095041
NO_TIE_POS = 2 ** 30
AUG_DEPTH = 256
KEY_CHUNK = 2048
I16 = jnp.int16
I16_MIN = -(2 ** 15)


def _dsa_kernel(q_ref, k_ref, v_ref, ki_ref, qit_ref, wit_ref, bias_ref, o_ref,
                kaug_ref, qaug_ref, key_ref, hi_ref, lo_ref, tie_ref, mask_ref, oh_ref, logit_ref, *, qb0, topk, hp):
    sc = v_ref.shape[0]
    rows = hp * Q_BLOCK
    i = pl.program_id(1) + qb0
    n0 = jnp.maximum(i - 1, 0)

    @pl.when(pl.program_id(1) == 0)
    def _():
        kaug_ref[:, Q_BLOCK:] = k_ref[...]
        r = lax.broadcasted_iota(I32, (rows, Q_BLOCK), 0)
        c = lax.broadcasted_iota(I32, (rows, Q_BLOCK), 1)
        eye = jnp.where(jnp.bitwise_and(r, Q_BLOCK - 1) == c, 1.0, 0.0).astype(BF16)
        for g in range(ATTN_HEADS // hp):
            qaug_ref[g, :, 0:Q_BLOCK] = eye
            qaug_ref[g, :, Q_BLOCK + ATTN_HEAD_DIM:] = jnp.zeros((rows, AUG_DEPTH - Q_BLOCK - ATTN_HEAD_DIM), BF16)

    kib = ki_ref[...]
    pad = jnp.zeros((128 - IDX_HEAD_DIM, 2 * Q_BLOCK), BF16)
    score = jnp.zeros((sc, Q_BLOCK), F32)
    for h in range(0, IDX_HEADS, 2):
        qh = jnp.concatenate(
            [qit_ref[h * IDX_HEAD_DIM:(h + 1) * IDX_HEAD_DIM, :], qit_ref[(h + 1) * IDX_HEAD_DIM:(h + 2) * IDX_HEAD_DIM, :]],
            axis=1).astype(BF16)
        rel = jnp.maximum(_dot(kib, jnp.concatenate([qh, pad], axis=0)), 0.0)
        score = score + rel[:, :Q_BLOCK] * wit_ref[h:h + 1, :] + rel[:, Q_BLOCK:] * wit_ref[h + 1:h + 2, :]
    score = score * ((IDX_HEADS ** -0.5) * (IDX_HEAD_DIM ** -0.5))
    kpos = lax.broadcasted_iota(I32, (sc, Q_BLOCK), 0)
    qpos = lax.broadcasted_iota(I32, (sc, Q_BLOCK), 1) + i * Q_BLOCK
    score = jnp.where(kpos <= qpos, score + 0.0, -jnp.inf)
    bits = pltpu.bitcast(score, I32)
    key = jnp.where(bits < 0, bits ^ 0x7FFFFFFF, bits)
    key_ref[...] = key
    hi_ref[...] = jnp.right_shift(key, 16).astype(I16)

    def half_search(ref, k_needed):
        def step(t, thr):
            cand = thr + jnp.left_shift(jnp.int32(1), 15 - t)
            cand16 = cand.astype(I16)
            cnt = _count(ref, lambda keys: keys >= cand16, packed=True)
            return jnp.where(cnt >= k_needed, cand, thr)

        return lax.fori_loop(0, 16, step, jnp.full((1, Q_BLOCK), I16_MIN, I32))

    thr_hi = half_search(hi_ref, float(topk))
    thr_hi16 = thr_hi.astype(I16)
    above = _count(hi_ref, lambda keys: keys > thr_hi16, packed=True)
    lo = (jnp.bitwise_and(key_ref[...], 0xFFFF) + I16_MIN).astype(I16)
    lo_ref[...] = jnp.where(hi_ref[...] == thr_hi16, lo, jnp.int16(I16_MIN))
    thr_lo = half_search(lo_ref, float(topk) - above)
    thr = thr_hi * 65536 + (thr_lo - I16_MIN)

    n_gt = _count(key_ref, lambda keys: keys > thr, packed=False)
    n_ge = _count(key_ref, lambda keys: keys >= thr, packed=False)
    need = topk - n_gt
    tie_ref[...] = jnp.where(key_ref[...] == thr, kpos, jnp.int32(NO_TIE_POS))
    at_neg_inf = thr == KEY_NEG_INF
    excess = jnp.where(jnp.logical_and(n_ge > topk, jnp.logical_not(at_neg_inf)), 1, 0)

    def index_search():
        def index_step(t, m):
            cand = m + jnp.left_shift(jnp.int32(1), 10 - t)
            cnt = _count(tie_ref, lambda pos: pos < cand, packed=False)
            return jnp.where(cnt < need, cand, m)

        return lax.fori_loop(0, 11, index_step, jnp.zeros((1, Q_BLOCK), I32))

    m = lax.cond(jnp.max(excess) > 0, index_search, lambda: jnp.full((1, Q_BLOCK), NO_TIE_POS - 1, I32))
    m = jnp.where(at_neg_inf, -1, m)
    mask = jnp.where(key_ref[...] > thr, 0.0, jnp.where(tie_ref[...] <= m, 0.0, NEG_BIG))
    mask_ref[...] = mask.astype(BF16)
    kaug_ref[:, 0:Q_BLOCK] = jnp.where(kpos < n0 * Q_BLOCK, mask, NEG_BIG).astype(BF16)

    q = q_ref[...] * (ATTN_HEAD_DIM ** -0.5)
    for h in range(ATTN_HEADS):
        r0 = (h % hp) * Q_BLOCK
        qaug_ref[h // hp, r0:r0 + Q_BLOCK, Q_BLOCK:Q_BLOCK + ATTN_HEAD_DIM] = (
            q[:, h * ATTN_HEAD_DIM:(h + 1) * ATTN_HEAD_DIM].astype(BF16))

    near0 = pl.multiple_of(n0 * Q_BLOCK, Q_BLOCK)
    kaug_near = jnp.concatenate(
        [mask_ref[pl.ds(near0, 2 * Q_BLOCK), :], kaug_ref[pl.ds(near0, 2 * Q_BLOCK), Q_BLOCK:]], axis=1)
    v_near = v_ref[pl.ds(near0, 2 * Q_BLOCK), :]
    pidx = jnp.minimum(i, 1)

    far = sc - 2 * Q_BLOCK
    near_chunks = [(True, 0, 2 * Q_BLOCK)] if far > 0 else [(True, 0, Q_BLOCK), (True, Q_BLOCK, 2 * Q_BLOCK)]
    chunks = [(False, c0, min(c0 + KEY_CHUNK, far)) for c0 in range(0, far, KEY_CHUNK)] + near_chunks
    n_groups = ATTN_HEADS // hp

    def logits_chunk(g, chunk):
        is_near, a, b = chunk
        if is_near:
            l = _dot_nt(qaug_ref[g], kaug_near[a:b, :]) + bias_ref[pidx, g, :, a:b]
            logit_ref[g % 2, :, sc + a:sc + b] = l
        else:
            l = _dot_nt(qaug_ref[g], kaug_ref[a:b, :])
            logit_ref[g % 2, :, a:b] = l
        return jnp.max(l, axis=1, keepdims=True)

    def values_chunk(g, chunk, mx):
        is_near, a, b = chunk
        if is_near:
            p = jnp.exp((logit_ref[g % 2, :, sc + a:sc + b] - mx).astype(BF16))
            return _dot(p, v_near[a:b, :])
        p = jnp.exp((logit_ref[g % 2, :, a:b] - mx).astype(BF16))
        return _dot(p, v_ref[a:b, :])

    def row_max(parts):
        while len(parts) > 1:
            parts = [jnp.maximum(parts[k], parts[k + 1]) if k + 1 < len(parts) else parts[k]
                     for k in range(0, len(parts), 2)]
        return parts[0]

    mx = row_max([logits_chunk(0, chunk) for chunk in chunks])
    for g in range(n_groups):
        acc = None
        next_parts = []
        for chunk in chunks:
            if g + 1 < n_groups:
                next_parts.append(logits_chunk(g + 1, chunk))
            part = values_chunk(g, chunk, mx)
            acc = part if acc is None else acc + part
        den = acc[:, ATTN_HEAD_DIM:ATTN_HEAD_DIM + 1]
        oh_ref[g] = acc[:, 0:ATTN_HEAD_DIM] / den
        if g + 1 < n_groups:
            mx = row_max(next_parts)
    o_ref[...] = jnp.concatenate(
        [oh_ref[h // hp, (h % hp) * Q_BLOCK:(h % hp + 1) * Q_BLOCK, :] for h in range(ATTN_HEADS)],
        axis=1).astype(BF16)


def dsa_attention(zrow, zkv, idx_t, bias_tiles, *, qb0, qb1, hp=2):
    b, s, _ = zrow.shape
    sc = qb1 * Q_BLOCK
    assert sc <= 2048, "the tie-break index search covers 11 bits"
    nq = qb1 - qb0
    nqs = s // Q_BLOCK
    ng = ATTN_HEADS // hp
    rows = hp * Q_BLOCK
    topk = min(TOPK_MAX, s // TOPK_DIVISOR)
    kv_spec = lambda slab: pl.BlockSpec((None, sc, 128), lambda bi, qi: (bi, 0, slab))
    return pl.pallas_call(
        functools.partial(_dsa_kernel, qb0=qb0, topk=topk, hp=hp),
        out_shape=jax.ShapeDtypeStruct((b, nq * Q_BLOCK, ATTN_WIDTH), BF16),
        grid=(b, nq),
        in_specs=[
            pl.BlockSpec((None, Q_BLOCK, ATTN_WIDTH), lambda bi, qi: (bi, qi + qb0, COL_Q // ATTN_WIDTH)),
            kv_spec(KV_SLAB_K),
            kv_spec(KV_SLAB_V),
            kv_spec(KV_SLAB_KI),
            pl.BlockSpec((IDX_HEADS * IDX_HEAD_DIM, Q_BLOCK), lambda bi, qi: (0, bi * nqs + qi + qb0)),
            pl.BlockSpec((16, Q_BLOCK), lambda bi, qi: (IDX_HEADS * IDX_HEAD_DIM // 16, bi * nqs + qi + qb0)),
            _resident((2, ng, rows, 2 * Q_BLOCK)),
        ],
        out_specs=pl.BlockSpec((None, Q_BLOCK, ATTN_WIDTH), lambda bi, qi: (bi, qi, 0)),
        scratch_shapes=[
            pltpu.VMEM((sc, AUG_DEPTH), BF16),
            pltpu.VMEM((ng, rows, AUG_DEPTH), BF16),
            pltpu.VMEM((sc, Q_BLOCK), I32),
            pltpu.VMEM((sc, Q_BLOCK), I16),
            pltpu.VMEM((sc, Q_BLOCK), I16),
            pltpu.VMEM((sc, Q_BLOCK), I32),
            pltpu.VMEM((sc, Q_BLOCK), BF16),
            pltpu.VMEM((ng, rows, ATTN_HEAD_DIM), F32),
            pltpu.VMEM((2, rows, sc + 2 * Q_BLOCK), F32),
        ],
        compiler_params=_params("parallel", "arbitrary"),
        name="dsa_attention",
    )(zrow, zkv, zkv, zkv, idx_t, idx_t, bias_tiles.reshape(2, ng, rows, 2 * Q_BLOCK))


def _merge_kernel(x_ref, gpre_ref, wg_ref, pool_ref, wpool_ref, attn_ref, wattn_ref, s5_ref, wglu_ref,
                  wout_ref, gpost_ref, o_ref):
    x = x_ref[...]
    h = _rms(x, gpre_ref[...]).astype(BF16)
    gates = jax.nn.sigmoid(_dot(h, wg_ref[...]))
    y_pool = _dot(pool_ref[...], wpool_ref[...])
    y_attn = _dot(attn_ref[...], wattn_ref[...])
    glu = _dot(s5_ref[...], wglu_ref[...])
    y_s5 = glu[:, :D_MODEL] * jax.nn.sigmoid(glu[:, D_MODEL:])
    merged = (gates[:, 0:D_MODEL] * y_pool + gates[:, D_MODEL:2 * D_MODEL] * y_attn
              + gates[:, 2 * D_MODEL:3 * D_MODEL] * y_s5)
    mixed = _dot(merged.astype(BF16), wout_ref[...])
    o_ref[...] = x + _rms(mixed, gpost_ref[...])


def merge(x, g_pre, w_gates, pool_y, w_pool, attn_y, w_attn, s5_y, w_glu, w_out, g_post, *, tm=256):
    t, d = x.shape
    row = lambda width: pl.BlockSpec((tm, width), lambda i: (i, 0))
    full = lambda a: _resident(a.shape)
    return pl.pallas_call(
        _merge_kernel,
        out_shape=jax.ShapeDtypeStruct((t, d), F32),
        grid=(t // tm,),
        in_specs=[row(d), full(g_pre), full(w_gates), row(POOL_WIDTH), full(w_pool), row(ATTN_WIDTH),
                  full(w_attn), row(S5_WIDTH), full(w_glu), full(w_out), full(g_post)],
        out_specs=row(d),
        compiler_params=_params("parallel"),
        name="merge",
    )(x, g_pre, w_gates, pool_y, w_pool, attn_y, w_attn, s5_y, w_glu, w_out, g_post)


def _ffn_kernel(x_ref, gpre_ref, win_ref, wdown_ref, gpost_ref, o_ref, *, th):
    x = x_ref[...]
    h = _rms(x, gpre_ref[...]).astype(BF16)
    hid = wdown_ref.shape[0]
    f = None
    for c0 in range(0, hid, th):
        gate = _dot(h, win_ref[:, c0:c0 + th])
        up = _dot(h, win_ref[:, hid + c0:hid + c0 + th])
        act = (jax.nn.silu(gate) * up).astype(BF16)
        part = _dot(act, wdown_ref[c0:c0 + th, :])
        f = part if f is None else f + part
    o_ref[...] = x + _rms(f, gpost_ref[...])


def ffn(x, g_pre, w_in, w_down, g_post, *, tm=512, th=256):
    t, d = x.shape
    return pl.pallas_call(
        functools.partial(_ffn_kernel, th=th),
        out_shape=jax.ShapeDtypeStruct((t, d), F32),
        grid=(t // tm,),
        in_specs=[
            pl.BlockSpec((tm, d), lambda i: (i, 0)),
            _resident((1, d)),
            _resident(w_in.shape),
            _resident(w_down.shape),
            _resident((1, d)),
        ],
        out_specs=pl.BlockSpec((tm, d), lambda i: (i, 0)),
        compiler_params=_params("parallel"),
        name="ffn",
    )(x, g_pre, w_in, w_down, g_post)


def _pad_cols(w, width):
    return jnp.pad(w, ((0, 0), (0, width - w.shape[1])))


def _split_w_in(w):
    sizes = (POOL_WIDTH, ATTN_WIDTH, ATTN_HEAD_DIM, ATTN_HEAD_DIM, IDX_HEADS * IDX_HEAD_DIM, IDX_HEAD_DIM,
             IDX_HEADS, S5_WIDTH, N_BRANCHES * D_MODEL)
    parts, start = [], 0
    for size in sizes:
        parts.append(w[:, start:start + size])
        start += size
    w_pool, w_q, w_k, w_v, w_qi, w_ki, w_wi, w_s5, w_gates = parts
    w_rows = jnp.concatenate(
        [w_pool, w_s5, w_q, _pad_cols(w_k, 128), _pad_cols(w_v, 128), _pad_cols(w_ki, 128)], axis=1)
    w_cols = jnp.concatenate([w_qi, _pad_cols(w_wi, IDX_T_ROWS - IDX_HEADS * IDX_HEAD_DIM)], axis=1).T
    return w_rows.astype(BF16), w_cols.astype(BF16), w_gates.astype(BF16)


def _block_diag(blocks):
    g, r, c = blocks.shape
    eye = jnp.eye(g, dtype=blocks.dtype)
    return (blocks[:, :, None, :] * eye[:, None, :, None]).reshape(g * r, g * c)


def _attn_classes(nqs):
    n_classes = 8 if nqs % 8 == 0 and nqs >= 16 else 1
    step = nqs // n_classes
    return [(c * step, (c + 1) * step) for c in range(n_classes)]


def kernel(x, norm_mix_pre, norm_mix_post, norm_ffn_pre, norm_ffn_post, w_in, pool_mix_w, pool_scale, pool_out_w, attn_out_w, rel_bias, s5_lambda_re, s5_lambda_im, s5_log_dt, s5_b_re, s5_b_im, s5_c_re, s5_c_im, s5_d, s5_glu_w, w_out, ffn_w_in, ffn_w_out):
    b, s, d = x.shape
    t = b * s
    depth = w_in.shape[0]
    bias_tiles = rel_bias_tiles(rel_bias)
    xf = x.reshape(t, d)
    for l in range(depth):
        w_rows, w_cols, w_gates = _split_w_in(w_in[l])
        g_pre = norm_mix_pre[l][None, :]
        zrow, zkv, idx_t = in_proj(xf, g_pre, w_rows, w_cols)
        zrow = zrow.reshape(b, s, COL_K)
        zkv = zkv.reshape(b, s, ROW_WIDTH - COL_K)

        pool_y = pool_mixer(zrow, pool_mix_w[l].astype(BF16), pool_scale[l][None, :])

        a_re, a_im, bb_re, bb_im = s5_discretise(
            s5_lambda_re[l][:, None, :], s5_lambda_im[l][:, None, :], s5_log_dt[l][:, None, None],
            jnp.swapaxes(s5_b_re[l], 1, 2), jnp.swapaxes(s5_b_im[l], 1, 2))
        s5_y = s5_scan(
            zrow,
            _block_diag(bb_re).astype(BF16), _block_diag(bb_im).astype(BF16),
            _block_diag(jnp.swapaxes(s5_c_re[l], 1, 2)).astype(BF16),
            _block_diag(jnp.swapaxes(s5_c_im[l], 1, 2)).astype(BF16),
            a_re.reshape(1, S5_LANES), a_im.reshape(1, S5_LANES), s5_d[l][None, :])

        attn_y = jnp.concatenate(
            [dsa_attention(zrow, zkv, idx_t, bias_tiles, qb0=q0, qb1=q1)
             for q0, q1 in _attn_classes(s // Q_BLOCK)], axis=1)

        xf = merge(xf, g_pre, w_gates, pool_y.reshape(t, POOL_WIDTH), pool_out_w[l].astype(BF16),
                   attn_y.reshape(t, ATTN_WIDTH), attn_out_w[l].astype(BF16),
                   s5_y.reshape(t, S5_WIDTH), s5_glu_w[l].astype(BF16), w_out[l].astype(BF16),
                   norm_mix_post[l][None, :])
        xf = ffn(xf, norm_ffn_pre[l][None, :], ffn_w_in[l].astype(BF16), ffn_w_out[l].astype(BF16),
                 norm_ffn_post[l][None, :])
    return xf.reshape(b, s, d)
```

```python
import functools
import math

import jax
import jax.numpy as jnp
from jax import lax
from jax.experimental import pallas as pl
from jax.experimental.pallas import tpu as pltpu

F32 = jnp.float32
BF16 = jnp.bfloat16
I32 = jnp.int32

D_MODEL = 1024
POOL_WINDOWS = (2, 4, 8, 16)
POOL_WIDTH = 512
POOL_GROUP = 128
POOL_HALO = 16
ATTN_HEADS = 16
ATTN_HEAD_DIM = 64
ATTN_WIDTH = 1024
IDX_HEADS = 8
IDX_HEAD_DIM = 64
TOPK_MAX = 256
TOPK_DIVISOR = 4
Q_BLOCK = 128
REL_BUCKETS = 32
REL_MAX_DISTANCE = 128
S5_WIDTH = 512
S5_GROUP = 16
S5_GROUPS = 32
S5_STATE = 64
S5_LANES = S5_GROUPS * S5_STATE
N_BRANCHES = 3
FFN_HIDDEN = 2816
NORM_EPS = 1e-6

COL_POOL = 0
COL_S5 = 512
COL_Q = 1024
COL_K = 2048
KV_SLAB_K, KV_SLAB_V, KV_SLAB_KI = 0, 1, 2
ROW_WIDTH = COL_K + 3 * 128
IDX_T_ROWS = 528

NEG_BIG = -1e30
INT_MIN = -(2 ** 31)

VMEM_LIMIT = 56 * 1024 * 1024


def _rms(x, g):
    ms = jnp.mean(x * x, axis=-1, keepdims=True)
    return x * lax.rsqrt(ms + NORM_EPS) * g


def _dot(a, b):
    return jnp.dot(a, b, preferred_element_type=F32)


def _dot_nt(a, b):
    return lax.dot_general(a, b, (((1,), (1,)), ((), ())), preferred_element_type=F32)


def _params(*sem):
    return pltpu.CompilerParams(dimension_semantics=sem, vmem_limit_bytes=VMEM_LIMIT)


def _resident(shape):
    return pl.BlockSpec(shape, lambda *_: (0,) * len(shape), pipeline_mode=pl.Buffered(1))


def _in_proj_kernel(x_ref, g_ref, w_ref, wt_ref, rows_ref, kv_ref, cols_ref):
    h = _rms(x_ref[...], g_ref[...]).astype(BF16)
    z = _dot(h, w_ref[...])
    rows_ref[...] = z[:, :COL_K]
    kv = z[:, COL_K:]
    lane = lax.broadcasted_iota(I32, kv.shape, 1)
    kv_ref[...] = jnp.where(lane == KV_SLAB_V * 128 + ATTN_HEAD_DIM, 1.0, kv).astype(BF16)
    cols_ref[...] = _dot_nt(wt_ref[...], h)


def in_proj(x, gain, w, wt, *, tm=512):
    t, d = x.shape
    n = w.shape[1]
    c = wt.shape[0]
    return pl.pallas_call(
        _in_proj_kernel,
        out_shape=(jax.ShapeDtypeStruct((t, COL_K), F32), jax.ShapeDtypeStruct((t, n - COL_K), BF16),
                   jax.ShapeDtypeStruct((c, t), F32)),
        grid=(t // tm,),
        in_specs=[
            pl.BlockSpec((tm, d), lambda i: (i, 0)),
            _resident((1, d)),
            _resident((d, n)),
            _resident((c, d)),
        ],
        out_specs=(pl.BlockSpec((tm, COL_K), lambda i: (i, 0)), pl.BlockSpec((tm, n - COL_K), lambda i: (i, 0)),
                   pl.BlockSpec((c, tm), lambda i: (0, i))),
        compiler_params=_params("parallel"),
        name="in_proj",
    )(x, gain, w, wt)


def _pool_kernel(u_ref, mixw_ref, scale_ref, o_ref, buf_ref):
    s = u_ref.shape[0]
    buf_ref[0:POOL_HALO, :] = jnp.zeros((POOL_HALO, POOL_WIDTH), F32)
    buf_ref[POOL_HALO:POOL_HALO + s, :] = u_ref[...]
    pos = lax.broadcasted_iota(I32, (s, POOL_GROUP), 0).astype(F32)
    for gi, w in enumerate(POOL_WINDOWS):
        c0 = gi * POOL_GROUP
        tok = buf_ref[POOL_HALO:POOL_HALO + s, c0:c0 + POOL_GROUP]
        acc = tok
        for k in range(1, w):
            acc = acc + buf_ref[POOL_HALO - k:POOL_HALO - k + s, c0:c0 + POOL_GROUP]
        count = jnp.minimum(pos + 1.0, float(w))
        d = acc / count - tok
        y = _dot(d.astype(BF16), mixw_ref[gi])
        o_ref[:, c0:c0 + POOL_GROUP] = (y * scale_ref[:, c0:c0 + POOL_GROUP]).astype(BF16)


def pool_mixer(zrow, mix_w, scale):
    b, s, _ = zrow.shape
    return pl.pallas_call(
        _pool_kernel,
        out_shape=jax.ShapeDtypeStruct((b, s, POOL_WIDTH), BF16),
        grid=(b,),
        in_specs=[
            pl.BlockSpec((None, s, POOL_WIDTH), lambda i: (i, 0, COL_POOL // POOL_WIDTH)),
            pl.BlockSpec((len(POOL_WINDOWS), POOL_GROUP, POOL_GROUP), lambda i: (0, 0, 0)),
            pl.BlockSpec((1, POOL_WIDTH), lambda i: (0, 0)),
        ],
        out_specs=pl.BlockSpec((None, s, POOL_WIDTH), lambda i: (i, 0, 0)),
        scratch_shapes=[pltpu.VMEM((POOL_HALO + s, POOL_WIDTH), F32)],
        compiler_params=_params("parallel"),
        name="pool_mixer",
    )(zrow, mix_w, scale)


def _s5_disc_kernel(lr_ref, li_ref, ldt_ref, br_ref, bi_ref, ar_ref, ai_ref, bbr_ref, bbi_ref):
    lr = lr_ref[...]
    li = li_ref[...]
    dt = jnp.exp(ldt_ref[...])
    mag = jnp.exp(lr * dt)
    a_re = mag * jnp.cos(li * dt)
    a_im = mag * jnp.sin(li * dt)
    den = lr * lr + li * li
    coef_re = ((a_re - 1.0) * lr + a_im * li) / den
    coef_im = (a_im * lr - (a_re - 1.0) * li) / den
    ar_ref[...] = a_re
    ai_ref[...] = a_im
    br = br_ref[...]
    bi = bi_ref[...]
    bbr_ref[...] = coef_re * br - coef_im * bi
    bbi_ref[...] = coef_re * bi + coef_im * br


def s5_discretise(lam_re, lam_im, log_dt, b_re_t, b_im_t):
    g, _, n = lam_re.shape
    j = b_re_t.shape[1]
    return pl.pallas_call(
        _s5_disc_kernel,
        out_shape=(
            jax.ShapeDtypeStruct((g, 1, n), F32),
            jax.ShapeDtypeStruct((g, 1, n), F32),
            jax.ShapeDtypeStruct((g, j, n), F32),
            jax.ShapeDtypeStruct((g, j, n), F32),
        ),
        name="s5_discretise",
    )(lam_re, lam_im, log_dt, b_re_t, b_im_t)


def _s5_scan_kernel(u_ref, bre_ref, bim_ref, cre_ref, cim_ref, ar_ref, ai_ref, d_ref, o_ref,
                    ut_ref, xr_ref, xi_ref, sr_ref, si_ref, y_ref, *, lane_chunk):
    nb, L, _ = u_ref.shape

    @pl.when(pl.program_id(0) == 0)
    def _():
        sr_ref[...] = jnp.zeros_like(sr_ref)
        si_ref[...] = jnp.zeros_like(si_ref)

    n_slab = S5_WIDTH // 128
    for b in range(nb):
        for c in range(n_slab):
            ut_ref[c, pl.ds(b, L, stride=nb), :] = u_ref[b, :, c * 128:(c + 1) * 128]
    u_t = jnp.concatenate([ut_ref[c] for c in range(n_slab)], axis=1)
    ub = u_t.astype(BF16)
    n_blk = S5_WIDTH // 128
    lanes_blk = S5_LANES // n_blk
    for k in range(n_blk):
        rs = slice(k * 128, (k + 1) * 128)
        ls = slice(k * lanes_blk, (k + 1) * lanes_blk)
        xr_ref[:, ls] = _dot(ub[:, rs], bre_ref[rs, ls])
        xi_ref[:, ls] = _dot(ub[:, rs], bim_ref[rs, ls])

    for c in range(S5_LANES // lane_chunk):
        cs = slice(c * lane_chunk, (c + 1) * lane_chunk)
        a_re = jnp.broadcast_to(ar_ref[:, cs], (nb, lane_chunk))
        a_im = jnp.broadcast_to(ai_ref[:, cs], (nb, lane_chunk))

        def step(t, carry, cs=cs, a_re=a_re, a_im=a_im):
            s_re, s_im = carry
            r0 = pl.multiple_of(t * nb, nb)
            n_re = a_re * s_re - a_im * s_im + xr_ref[pl.ds(r0, nb), cs]
            n_im = a_re * s_im + a_im * s_re + xi_ref[pl.ds(r0, nb), cs]
            xr_ref[pl.ds(r0, nb), cs] = n_re
            xi_ref[pl.ds(r0, nb), cs] = n_im
            return n_re, n_im

        s_re, s_im = lax.fori_loop(0, L, step, (sr_ref[:, cs], si_ref[:, cs]), unroll=2)
        sr_ref[:, cs] = s_re
        si_ref[:, cs] = s_im

    for k in range(n_blk):
        rs = slice(k * 128, (k + 1) * 128)
        ls = slice(k * lanes_blk, (k + 1) * lanes_blk)
        y = _dot(xr_ref[:, ls].astype(BF16), cre_ref[ls, rs]) - _dot(xi_ref[:, ls].astype(BF16), cim_ref[ls, rs])
        y_ref[k] = jax.nn.gelu(y + d_ref[:, rs] * u_t[:, rs])
    for b in range(nb):
        for c in range(n_slab):
            o_ref[b, :, c * 128:(c + 1) * 128] = y_ref[c, pl.ds(b, L, stride=nb), :].astype(BF16)


def s5_scan(zrow, b_re_full, b_im_full, c_re_full, c_im_full, a_re, a_im, d_skip, *, chunk=64, lane_chunk=1024):
    b, s, _ = zrow.shape
    rows = chunk * b
    return pl.pallas_call(
        functools.partial(_s5_scan_kernel, lane_chunk=lane_chunk),
        out_shape=jax.ShapeDtypeStruct((b, s, S5_WIDTH), BF16),
        grid=(s // chunk,),
        in_specs=[
            pl.BlockSpec((b, chunk, S5_WIDTH), lambda i: (0, i, COL_S5 // S5_WIDTH)),
            _resident((S5_WIDTH, S5_LANES)),
            _resident((S5_WIDTH, S5_LANES)),
            _resident((S5_LANES, S5_WIDTH)),
            _resident((S5_LANES, S5_WIDTH)),
            _resident((1, S5_LANES)),
            _resident((1, S5_LANES)),
            _resident((1, S5_WIDTH)),
        ],
        out_specs=pl.BlockSpec((b, chunk, S5_WIDTH), lambda i: (0, i, 0)),
        scratch_shapes=[
            pltpu.VMEM((S5_WIDTH // 128, rows, 128), F32),
            pltpu.VMEM((rows, S5_LANES), F32),
            pltpu.VMEM((rows, S5_LANES), F32),
            pltpu.VMEM((b, S5_LANES), F32),
            pltpu.VMEM((b, S5_LANES), F32),
            pltpu.VMEM((S5_WIDTH // 128, rows, 128), F32),
        ],
        compiler_params=_params("arbitrary"),
        name="s5_scan",
    )(zrow, b_re_full, b_im_full, c_re_full, c_im_full, a_re, a_im, d_skip)


def _rel_bias_kernel(rb_ref, o_ref):
    p = pl.program_id(0)
    h = pl.program_id(1)
    ql = lax.broadcasted_iota(I32, (Q_BLOCK, 2 * Q_BLOCK), 0)
    c = lax.broadcasted_iota(I32, (Q_BLOCK, 2 * Q_BLOCK), 1)
    dist = jnp.maximum(ql - c + p * Q_BLOCK, 0)
    max_exact = REL_BUCKETS // 2
    d_f = jnp.maximum(dist, 1).astype(F32)
    large = max_exact + (jnp.log(d_f / max_exact) / math.log(REL_MAX_DISTANCE / max_exact)
                         * (REL_BUCKETS - max_exact)).astype(I32)
    large = jnp.minimum(large, REL_BUCKETS - 1)
    bucket = jnp.where(dist < max_exact, dist, large)
    acc = jnp.zeros((Q_BLOCK, 2 * Q_BLOCK), F32)
    for k in range(REL_BUCKETS):
        acc = jnp.where(bucket == k, rb_ref[k, h], acc)
    o_ref[...] = acc - rb_ref[REL_BUCKETS - 1, h]


def rel_bias_tiles(rel_bias):
    return pl.pallas_call(
        _rel_bias_kernel,
        out_shape=jax.ShapeDtypeStruct((2, ATTN_HEADS, Q_BLOCK, 2 * Q_BLOCK), F32),
        grid=(2, ATTN_HEADS),
        in_specs=[pl.BlockSpec(memory_space=pltpu.SMEM)],
        out_specs=pl.BlockSpec((None, None, Q_BLOCK, 2 * Q_BLOCK), lambda p, h: (p, h, 0, 0)),
        name="rel_bias_tiles",
    )(rel_bias)


COUNT_CHAINS = 8


def _count(ref, pred, *, packed):
    rows = 16 if packed else 8
    one, zero = (jnp.int16(1), jnp.int16(0)) if packed else (1.0, 0.0)
    n = ref.shape[0] // rows
    chains = min(COUNT_CHAINS, n)
    accs = []
    for j in range(n):
        hit = jnp.where(pred(ref[j * rows:(j + 1) * rows, :]), one, zero)
        if j < chains:
            accs.append(hit)
        else:
            accs[j % chains] = accs[j % chains] + hit
    while len(accs) > 1:
        accs = [accs[c] + accs[c + len(accs) // 2] for c in range(len(accs) // 2)]
    return jnp.sum(accs[0].astype(F32), axis=0, keepdims=True)


KEY_NEG_INF = -2139095041
NO_TIE_POS = 2 ** 30
AUG_DEPTH = 256
KEY_CHUNK = 2048
I16 = jnp.int16
I16_MIN = -(2 ** 15)


def _dsa_kernel(q_ref, k_ref, v_ref, ki_ref, qit_ref, wit_ref, bias_ref, o_ref,
                kaug_ref, qaug_ref, key_ref, hi_ref, lo_ref, tie_ref, mask_ref, oh_ref, logit_ref, *, qb0, topk, hp):
    sc = v_ref.shape[0]
    rows = hp * Q_BLOCK
    i = pl.program_id(1) + qb0
    n0 = jnp.maximum(i - 1, 0)

    @pl.when(pl.program_id(1) == 0)
    def _():
        kaug_ref[:, Q_BLOCK:] = k_ref[...]
        r = lax.broadcasted_iota(I32, (rows, Q_BLOCK), 0)
        c = lax.broadcasted_iota(I32, (rows, Q_BLOCK), 1)
        eye = jnp.where(jnp.bitwise_and(r, Q_BLOCK - 1) == c, 1.0, 0.0).astype(BF16)
        for g in range(ATTN_HEADS // hp):
            qaug_ref[g, :, 0:Q_BLOCK] = eye
            qaug_ref[g, :, Q_BLOCK + ATTN_HEAD_DIM:] = jnp.zeros((rows, AUG_DEPTH - Q_BLOCK - ATTN_HEAD_DIM), BF16)

    kib = ki_ref[...]
    pad = jnp.zeros((128 - IDX_HEAD_DIM, 2 * Q_BLOCK), BF16)
    score = jnp.zeros((sc, Q_BLOCK), F32)
    for h in range(0, IDX_HEADS, 2):
        qh = jnp.concatenate(
            [qit_ref[h * IDX_HEAD_DIM:(h + 1) * IDX_HEAD_DIM, :], qit_ref[(h + 1) * IDX_HEAD_DIM:(h + 2) * IDX_HEAD_DIM, :]],
            axis=1).astype(BF16)
        rel = jnp.maximum(_dot(kib, jnp.concatenate([qh, pad], axis=0)), 0.0)
        score = score + rel[:, :Q_BLOCK] * wit_ref[h:h + 1, :] + rel[:, Q_BLOCK:] * wit_ref[h + 1:h + 2, :]
    score = score * ((IDX_HEADS ** -0.5) * (IDX_HEAD_DIM ** -0.5))
    kpos = lax.broadcasted_iota(I32, (sc, Q_BLOCK), 0)
    always_causal = qb0 * Q_BLOCK
    score = score + 0.0
    tail_shape = (sc - always_causal, Q_BLOCK)
    tail_kpos = lax.broadcasted_iota(I32, tail_shape, 0) + always_causal
    tail_qpos = lax.broadcasted_iota(I32, tail_shape, 1) + i * Q_BLOCK
    tail = jnp.where(tail_kpos <= tail_qpos, score[always_causal:], -jnp.inf)
    score = jnp.concatenate([score[:always_causal], tail], axis=0) if always_causal else tail
    bits = pltpu.bitcast(score, I32)
    key = jnp.where(bits < 0, bits ^ 0x7FFFFFFF, bits)
    key_ref[...] = key
    hi_ref[...] = jnp.right_shift(key, 16).astype(I16)

    def half_search(ref, k_needed):
        def step(t, thr):
            cand = thr + jnp.left_shift(jnp.int32(1), 15 - t)
            cand16 = cand.astype(I16)
            cnt = _count(ref, lambda keys: keys >= cand16, packed=True)
            return jnp.where(cnt >= k_needed, cand, thr)

        return lax.fori_loop(0, 16, step, jnp.full((1, Q_BLOCK), I16_MIN, I32))

    thr_hi = half_search(hi_ref, float(topk))
    thr_hi16 = thr_hi.astype(I16)
    above = _count(hi_ref, lambda keys: keys > thr_hi16, packed=True)
    lo = (jnp.bitwise_and(key_ref[...], 0xFFFF) + I16_MIN).astype(I16)
    lo_ref[...] = jnp.where(hi_ref[...] == thr_hi16, lo, jnp.int16(I16_MIN))
    thr_lo = half_search(lo_ref, float(topk) - above)
    thr = thr_hi * 65536 + (thr_lo - I16_MIN)

    n_gt = _count(key_ref, lambda keys: keys > thr, packed=False)
    n_ge = _count(key_ref, lambda keys: keys >= thr, packed=False)
    need = topk - n_gt
    tie_ref[...] = jnp.where(key_ref[...] == thr, kpos, jnp.int32(NO_TIE_POS))
    at_neg_inf = thr == KEY_NEG_INF
    excess = jnp.where(jnp.logical_and(n_ge > topk, jnp.logical_not(at_neg_inf)), 1, 0)

    def index_search():
        def index_step(t, m):
            cand = m + jnp.left_shift(jnp.int32(1), 10 - t)
            cnt = _count(tie_ref, lambda pos: pos < cand, packed=False)
            return jnp.where(cnt < need, cand, m)

        return lax.fori_loop(0, 11, index_step, jnp.zeros((1, Q_BLOCK), I32))

    m = lax.cond(jnp.max(excess) > 0, index_search, lambda: jnp.full((1, Q_BLOCK), NO_TIE_POS - 1, I32))
    m = jnp.where(at_neg_inf, -1, m)
    mask = jnp.where(key_ref[...] > thr, 0.0, jnp.where(tie_ref[...] <= m, 0.0, NEG_BIG))
    mask_ref[...] = mask.astype(BF16)
    always_far = max(qb0 - 1, 0) * Q_BLOCK
    if always_far:
        kaug_ref[0:always_far, 0:Q_BLOCK] = mask[:always_far].astype(BF16)
    rest_kpos = lax.broadcasted_iota(I32, (sc - always_far, Q_BLOCK), 0) + always_far
    kaug_ref[always_far:, 0:Q_BLOCK] = jnp.where(rest_kpos < n0 * Q_BLOCK, mask[always_far:], NEG_BIG).astype(BF16)

    q = q_ref[...] * (ATTN_HEAD_DIM ** -0.5)
    for h in range(ATTN_HEADS):
        r0 = (h % hp) * Q_BLOCK
        qaug_ref[h // hp, r0:r0 + Q_BLOCK, Q_BLOCK:Q_BLOCK + ATTN_HEAD_DIM] = (
            q[:, h * ATTN_HEAD_DIM:(h + 1) * ATTN_HEAD_DIM].astype(BF16))

    near0 = pl.multiple_of(n0 * Q_BLOCK, Q_BLOCK)
    kaug_near = jnp.concatenate(
        [mask_ref[pl.ds(near0, 2 * Q_BLOCK), :], kaug_ref[pl.ds(near0, 2 * Q_BLOCK), Q_BLOCK:]], axis=1)
    v_near = v_ref[pl.ds(near0, 2 * Q_BLOCK), :]
    pidx = jnp.minimum(i, 1)

    far = sc - 2 * Q_BLOCK
    near_chunks = [(True, 0, 2 * Q_BLOCK)] if far > 0 else [(True, 0, Q_BLOCK), (True, Q_BLOCK, 2 * Q_BLOCK)]
    chunks = [(False, c0, min(c0 + KEY_CHUNK, far)) for c0 in range(0, far, KEY_CHUNK)] + near_chunks
    n_groups = ATTN_HEADS // hp

    def logits_chunk(g, chunk):
        is_near, a, b = chunk
        if is_near:
            l = _dot_nt(qaug_ref[g], kaug_near[a:b, :]) + bias_ref[pidx, g, :, a:b]
            logit_ref[g % 2, :, sc + a:sc + b] = l
        else:
            l = _dot_nt(qaug_ref[g], kaug_ref[a:b, :])
            logit_ref[g % 2, :, a:b] = l
        return jnp.max(l, axis=1, keepdims=True)

    def values_chunk(g, chunk, mx):
        is_near, a, b = chunk
        if is_near:
            p = jnp.exp((logit_ref[g % 2, :, sc + a:sc + b] - mx).astype(BF16))
            return _dot(p, v_near[a:b, :])
        p = jnp.exp((logit_ref[g % 2, :, a:b] - mx).astype(BF16))
        return _dot(p, v_ref[a:b, :])

    def row_max(parts):
        while len(parts) > 1:
            parts = [jnp.maximum(parts[k], parts[k + 1]) if k + 1 < len(parts) else parts[k]
                     for k in range(0, len(parts), 2)]
        return parts[0]

    mx = row_max([logits_chunk(0, chunk) for chunk in chunks])
    for g in range(n_groups):
        acc = None
        next_parts = []
        for chunk in chunks:
            if g + 1 < n_groups:
                next_parts.append(logits_chunk(g + 1, chunk))
            part = values_chunk(g, chunk, mx)
            acc = part if acc is None else acc + part
        den = acc[:, ATTN_HEAD_DIM:ATTN_HEAD_DIM + 1]
        oh_ref[g] = acc[:, 0:ATTN_HEAD_DIM] / den
        if g + 1 < n_groups:
            mx = row_max(next_parts)
    o_ref[...] = jnp.concatenate(
        [oh_ref[h // hp, (h % hp) * Q_BLOCK:(h % hp + 1) * Q_BLOCK, :] for h in range(ATTN_HEADS)],
        axis=1).astype(BF16)


def dsa_attention(zrow, zkv, idx_t, bias_tiles, *, qb0, qb1, hp=2):
    b, s, _ = zrow.shape
    sc = qb1 * Q_BLOCK
    assert sc <= 2048, "the tie-break index search covers 11 bits"
    nq = qb1 - qb0
    nqs = s // Q_BLOCK
    ng = ATTN_HEADS // hp
    rows = hp * Q_BLOCK
    topk = min(TOPK_MAX, s // TOPK_DIVISOR)
    kv_spec = lambda slab: pl.BlockSpec((None, sc, 128), lambda bi, qi: (bi, 0, slab))
    return pl.pallas_call(
        functools.partial(_dsa_kernel, qb0=qb0, topk=topk, hp=hp),
        out_shape=jax.ShapeDtypeStruct((b, nq * Q_BLOCK, ATTN_WIDTH), BF16),
        grid=(b, nq),
        in_specs=[
            pl.BlockSpec((None, Q_BLOCK, ATTN_WIDTH), lambda bi, qi: (bi, qi + qb0, COL_Q // ATTN_WIDTH)),
            kv_spec(KV_SLAB_K),
            kv_spec(KV_SLAB_V),
            kv_spec(KV_SLAB_KI),
            pl.BlockSpec((IDX_HEADS * IDX_HEAD_DIM, Q_BLOCK), lambda bi, qi: (0, bi * nqs + qi + qb0)),
            pl.BlockSpec((16, Q_BLOCK), lambda bi, qi: (IDX_HEADS * IDX_HEAD_DIM // 16, bi * nqs + qi + qb0)),
            _resident((2, ng, rows, 2 * Q_BLOCK)),
        ],
        out_specs=pl.BlockSpec((None, Q_BLOCK, ATTN_WIDTH), lambda bi, qi: (bi, qi, 0)),
        scratch_shapes=[
            pltpu.VMEM((sc, AUG_DEPTH), BF16),
            pltpu.VMEM((ng, rows, AUG_DEPTH), BF16),
            pltpu.VMEM((sc, Q_BLOCK), I32),
            pltpu.VMEM((sc, Q_BLOCK), I16),
            pltpu.VMEM((sc, Q_BLOCK), I16),
            pltpu.VMEM((sc, Q_BLOCK), I32),
            pltpu.VMEM((sc, Q_BLOCK), BF16),
            pltpu.VMEM((ng, rows, ATTN_HEAD_DIM), F32),
            pltpu.VMEM((2, rows, sc + 2 * Q_BLOCK), F32),
        ],
        compiler_params=_params("parallel", "arbitrary"),
        name="dsa_attention",
    )(zrow, zkv, zkv, zkv, idx_t, idx_t, bias_tiles.reshape(2, ng, rows, 2 * Q_BLOCK))


def _merge_kernel(x_ref, gpre_ref, wg_ref, pool_ref, wpool_ref, attn_ref, wattn_ref, s5_ref, wglu_ref,
                  wout_ref, gpost_ref, o_ref):
    x = x_ref[...]
    h = _rms(x, gpre_ref[...]).astype(BF16)
    gates = jax.nn.sigmoid(_dot(h, wg_ref[...]))
    y_pool = _dot(pool_ref[...], wpool_ref[...])
    y_attn = _dot(attn_ref[...], wattn_ref[...])
    glu = _dot(s5_ref[...], wglu_ref[...])
    y_s5 = glu[:, :D_MODEL] * jax.nn.sigmoid(glu[:, D_MODEL:])
    merged = (gates[:, 0:D_MODEL] * y_pool + gates[:, D_MODEL:2 * D_MODEL] * y_attn
              + gates[:, 2 * D_MODEL:3 * D_MODEL] * y_s5)
    mixed = _dot(merged.astype(BF16), wout_ref[...])
    o_ref[...] = x + _rms(mixed, gpost_ref[...])


def merge(x, g_pre, w_gates, pool_y, w_pool, attn_y, w_attn, s5_y, w_glu, w_out, g_post, *, tm=256):
    t, d = x.shape
    row = lambda width: pl.BlockSpec((tm, width), lambda i: (i, 0))
    full = lambda a: _resident(a.shape)
    return pl.pallas_call(
        _merge_kernel,
        out_shape=jax.ShapeDtypeStruct((t, d), F32),
        grid=(t // tm,),
        in_specs=[row(d), full(g_pre), full(w_gates), row(POOL_WIDTH), full(w_pool), row(ATTN_WIDTH),
                  full(w_attn), row(S5_WIDTH), full(w_glu), full(w_out), full(g_post)],
        out_specs=row(d),
        compiler_params=_params("parallel"),
        name="merge",
    )(x, g_pre, w_gates, pool_y, w_pool, attn_y, w_attn, s5_y, w_glu, w_out, g_post)


def _ffn_kernel(x_ref, gpre_ref, win_ref, wdown_ref, gpost_ref, o_ref, *, th):
    x = x_ref[...]
    h = _rms(x, gpre_ref[...]).astype(BF16)
    hid = wdown_ref.shape[0]
    f = None
    for c0 in range(0, hid, th):
        gate = _dot(h, win_ref[:, c0:c0 + th])
        up = _dot(h, win_ref[:, hid + c0:hid + c0 + th])
        act = (jax.nn.silu(gate) * up).astype(BF16)
        part = _dot(act, wdown_ref[c0:c0 + th, :])
        f = part if f is None else f + part
    o_ref[...] = x + _rms(f, gpost_ref[...])


def ffn(x, g_pre, w_in, w_down, g_post, *, tm=512, th=256):
    t, d = x.shape
    return pl.pallas_call(
        functools.partial(_ffn_kernel, th=th),
        out_shape=jax.ShapeDtypeStruct((t, d), F32),
        grid=(t // tm,),
        in_specs=[
            pl.BlockSpec((tm, d), lambda i: (i, 0)),
            _resident((1, d)),
            _resident(w_in.shape),
            _resident(w_down.shape),
            _resident((1, d)),
        ],
        out_specs=pl.BlockSpec((tm, d), lambda i: (i, 0)),
        compiler_params=_params("parallel"),
        name="ffn",
    )(x, g_pre, w_in, w_down, g_post)


def _pad_cols(w, width):
    return jnp.pad(w, ((0, 0), (0, width - w.shape[1])))


def _split_w_in(w):
    sizes = (POOL_WIDTH, ATTN_WIDTH, ATTN_HEAD_DIM, ATTN_HEAD_DIM, IDX_HEADS * IDX_HEAD_DIM, IDX_HEAD_DIM,
             IDX_HEADS, S5_WIDTH, N_BRANCHES * D_MODEL)
    parts, start = [], 0
    for size in sizes:
        parts.append(w[:, start:start + size])
        start += size
    w_pool, w_q, w_k, w_v, w_qi, w_ki, w_wi, w_s5, w_gates = parts
    w_rows = jnp.concatenate(
        [w_pool, w_s5, w_q, _pad_cols(w_k, 128), _pad_cols(w_v, 128), _pad_cols(w_ki, 128)], axis=1)
    w_cols = jnp.concatenate([w_qi, _pad_cols(w_wi, IDX_T_ROWS - IDX_HEADS * IDX_HEAD_DIM)], axis=1).T
    return w_rows.astype(BF16), w_cols.astype(BF16), w_gates.astype(BF16)


def _block_diag(blocks):
    g, r, c = blocks.shape
    eye = jnp.eye(g, dtype=blocks.dtype)
    return (blocks[:, :, None, :] * eye[:, None, :, None]).reshape(g * r, g * c)


def _attn_classes(nqs):
    n_classes = 8 if nqs % 8 == 0 and nqs >= 16 else 1
    step = nqs // n_classes
    return [(c * step, (c + 1) * step) for c in range(n_classes)]


def kernel(x, norm_mix_pre, norm_mix_post, norm_ffn_pre, norm_ffn_post, w_in, pool_mix_w, pool_scale, pool_out_w, attn_out_w, rel_bias, s5_lambda_re, s5_lambda_im, s5_log_dt, s5_b_re, s5_b_im, s5_c_re, s5_c_im, s5_d, s5_glu_w, w_out, ffn_w_in, ffn_w_out):
    b, s, d = x.shape
    t = b * s
    depth = w_in.shape[0]
    bias_tiles = rel_bias_tiles(rel_bias)
    xf = x.reshape(t, d)
    for l in range(depth):
        w_rows, w_cols, w_gates = _split_w_in(w_in[l])
        g_pre = norm_mix_pre[l][None, :]
        zrow, zkv, idx_t = in_proj(xf, g_pre, w_rows, w_cols)
        zrow = zrow.reshape(b, s, COL_K)
        zkv = zkv.reshape(b, s, ROW_WIDTH - COL_K)

        pool_y = pool_mixer(zrow, pool_mix_w[l].astype(BF16), pool_scale[l][None, :])

        a_re, a_im, bb_re, bb_im = s5_discretise(
            s5_lambda_re[l][:, None, :], s5_lambda_im[l][:, None, :], s5_log_dt[l][:, None, None],
            jnp.swapaxes(s5_b_re[l], 1, 2), jnp.swapaxes(s5_b_im[l], 1, 2))
        s5_y = s5_scan(
            zrow,
            _block_diag(bb_re).astype(BF16), _block_diag(bb_im).astype(BF16),
            _block_diag(jnp.swapaxes(s5_c_re[l], 1, 2)).astype(BF16),
            _block_diag(jnp.swapaxes(s5_c_im[l], 1, 2)).astype(BF16),
            a_re.reshape(1, S5_LANES), a_im.reshape(1, S5_LANES), s5_d[l][None, :])

        attn_y = jnp.concatenate(
            [dsa_attention(zrow, zkv, idx_t, bias_tiles, qb0=q0, qb1=q1)
             for q0, q1 in _attn_classes(s // Q_BLOCK)], axis=1)

        xf = merge(xf, g_pre, w_gates, pool_y.reshape(t, POOL_WIDTH), pool_out_w[l].astype(BF16),
                   attn_y.reshape(t, ATTN_WIDTH), attn_out_w[l].astype(BF16),
                   s5_y.reshape(t, S5_WIDTH), s5_glu_w[l].astype(BF16), w_out[l].astype(BF16),
                   norm_mix_post[l][None, :])
        xf = ffn(xf, norm_ffn_pre[l][None, :], ffn_w_in[l].astype(BF16), ffn_w_out[l].astype(BF16),
                 norm_ffn_post[l][None, :])
    return xf.reshape(b, s, d)
```

```python
import functools
import math

import jax
import jax.numpy as jnp
from jax import lax
from jax.experimental import pallas as pl
from jax.experimental.pallas import tpu as pltpu

F32 = jnp.float32
BF16 = jnp.bfloat16
I32 = jnp.int32

D_MODEL = 1024
POOL_WINDOWS = (2, 4, 8, 16)
POOL_WIDTH = 512
POOL_GROUP = 128
POOL_HALO = 16
ATTN_HEADS = 16
ATTN_HEAD_DIM = 64
ATTN_WIDTH = 1024
IDX_HEADS = 8
IDX_HEAD_DIM = 64
TOPK_MAX = 256
TOPK_DIVISOR = 4
Q_BLOCK = 128
REL_BUCKETS = 32
REL_MAX_DISTANCE = 128
S5_WIDTH = 512
S5_GROUP = 16
S5_GROUPS = 32
S5_STATE = 64
S5_LANES = S5_GROUPS * S5_STATE
N_BRANCHES = 3
FFN_HIDDEN = 2816
NORM_EPS = 1e-6

COL_POOL = 0
COL_S5 = 512
COL_Q = 1024
COL_K = 2048
KV_SLAB_K, KV_SLAB_V, KV_SLAB_KI = 0, 1, 2
ROW_WIDTH = COL_K + 3 * 128
IDX_T_ROWS = 528

NEG_BIG = -1e30
INT_MIN = -(2 ** 31)

VMEM_LIMIT = 56 * 1024 * 1024


def _rms(x, g):
    ms = jnp.mean(x * x, axis=-1, keepdims=True)
    return x * lax.rsqrt(ms + NORM_EPS) * g


def _dot(a, b):
    return jnp.dot(a, b, preferred_element_type=F32)


def _dot_nt(a, b):
    return lax.dot_general(a, b, (((1,), (1,)), ((), ())), preferred_element_type=F32)


def _params(*sem):
    return pltpu.CompilerParams(dimension_semantics=sem, vmem_limit_bytes=VMEM_LIMIT)


def _resident(shape):
    return pl.BlockSpec(shape, lambda *_: (0,) * len(shape), pipeline_mode=pl.Buffered(1))


def _in_proj_kernel(x_ref, g_ref, w_ref, wt_ref, rows_ref, kv_ref, cols_ref):
    h = _rms(x_ref[...], g_ref[...]).astype(BF16)
    z = _dot(h, w_ref[...])
    rows_ref[...] = z[:, :COL_K]
    kv = z[:, COL_K:]
    lane = lax.broadcasted_iota(I32, kv.shape, 1)
    kv_ref[...] = jnp.where(lane == KV_SLAB_V * 128 + ATTN_HEAD_DIM, 1.0, kv).astype(BF16)
    cols_ref[...] = _dot_nt(wt_ref[...], h)


def in_proj(x, gain, w, wt, *, tm=512):
    t, d = x.shape
    n = w.shape[1]
    c = wt.shape[0]
    return pl.pallas_call(
        _in_proj_kernel,
        out_shape=(jax.ShapeDtypeStruct((t, COL_K), F32), jax.ShapeDtypeStruct((t, n - COL_K), BF16),
                   jax.ShapeDtypeStruct((c, t), F32)),
        grid=(t // tm,),
        in_specs=[
            pl.BlockSpec((tm, d), lambda i: (i, 0)),
            _resident((1, d)),
            _resident((d, n)),
            _resident((c, d)),
        ],
        out_specs=(pl.BlockSpec((tm, COL_K), lambda i: (i, 0)), pl.BlockSpec((tm, n - COL_K), lambda i: (i, 0)),
                   pl.BlockSpec((c, tm), lambda i: (0, i))),
        compiler_params=_params("parallel"),
        name="in_proj",
    )(x, gain, w, wt)


def _pool_kernel(u_ref, mixw_ref, scale_ref, o_ref, buf_ref):
    s = u_ref.shape[0]
    buf_ref[0:POOL_HALO, :] = jnp.zeros((POOL_HALO, POOL_GROUP), F32)
    pos = lax.broadcasted_iota(I32, (s, POOL_GROUP), 0).astype(F32)
    for gi, w in enumerate(POOL_WINDOWS):
        assert w & (w - 1) == 0 and w <= POOL_HALO
        c0 = gi * POOL_GROUP
        tok = u_ref[:, c0:c0 + POOL_GROUP]
        acc = tok
        shift = 1
        while shift < w:
            buf_ref[POOL_HALO:POOL_HALO + s, :] = acc
            acc = acc + buf_ref[POOL_HALO - shift:POOL_HALO - shift + s, :]
            shift *= 2
        count = jnp.minimum(pos + 1.0, float(w))
        d = acc / count - tok
        y = _dot(d.astype(BF16), mixw_ref[gi])
        o_ref[:, c0:c0 + POOL_GROUP] = (y * scale_ref[:, c0:c0 + POOL_GROUP]).astype(BF16)


def pool_mixer(zrow, mix_w, scale):
    b, s, _ = zrow.shape
    return pl.pallas_call(
        _pool_kernel,
        out_shape=jax.ShapeDtypeStruct((b, s, POOL_WIDTH), BF16),
        grid=(b,),
        in_specs=[
            pl.BlockSpec((None, s, POOL_WIDTH), lambda i: (i, 0, COL_POOL // POOL_WIDTH)),
            pl.BlockSpec((len(POOL_WINDOWS), POOL_GROUP, POOL_GROUP), lambda i: (0, 0, 0)),
            pl.BlockSpec((1, POOL_WIDTH), lambda i: (0, 0)),
        ],
        out_specs=pl.BlockSpec((None, s, POOL_WIDTH), lambda i: (i, 0, 0)),
        scratch_shapes=[pltpu.VMEM((POOL_HALO + s, POOL_GROUP), F32)],
        compiler_params=_params("parallel"),
        name="pool_mixer",
    )(zrow, mix_w, scale)


def _s5_disc_kernel(lr_ref, li_ref, ldt_ref, br_ref, bi_ref, ar_ref, ai_ref, bbr_ref, bbi_ref):
    lr = lr_ref[...]
    li = li_ref[...]
    dt = jnp.exp(ldt_ref[...])
    mag = jnp.exp(lr * dt)
    a_re = mag * jnp.cos(li * dt)
    a_im = mag * jnp.sin(li * dt)
    den = lr * lr + li * li
    coef_re = ((a_re - 1.0) * lr + a_im * li) / den
    coef_im = (a_im * lr - (a_re - 1.0) * li) / den
    ar_ref[...] = a_re
    ai_ref[...] = a_im
    br = br_ref[...]
    bi = bi_ref[...]
    bbr_ref[...] = coef_re * br - coef_im * bi
    bbi_ref[...] = coef_re * bi + coef_im * br


def s5_discretise(lam_re, lam_im, log_dt, b_re_t, b_im_t):
    g, _, n = lam_re.shape
    j = b_re_t.shape[1]
    return pl.pallas_call(
        _s5_disc_kernel,
        out_shape=(
            jax.ShapeDtypeStruct((g, 1, n), F32),
            jax.ShapeDtypeStruct((g, 1, n), F32),
            jax.ShapeDtypeStruct((g, j, n), F32),
            jax.ShapeDtypeStruct((g, j, n), F32),
        ),
        name="s5_discretise",
    )(lam_re, lam_im, log_dt, b_re_t, b_im_t)


def _s5_scan_kernel(u_ref, bre_ref, bim_ref, cre_ref, cim_ref, ar_ref, ai_ref, d_ref, o_ref,
                    ut_ref, xr_ref, xi_ref, sr_ref, si_ref, y_ref, *, lane_chunk):
    nb, L, _ = u_ref.shape

    @pl.when(pl.program_id(0) == 0)
    def _():
        sr_ref[...] = jnp.zeros_like(sr_ref)
        si_ref[...] = jnp.zeros_like(si_ref)

    n_slab = S5_WIDTH // 128
    for b in range(nb):
        for c in range(n_slab):
            ut_ref[c, pl.ds(b, L, stride=nb), :] = u_ref[b, :, c * 128:(c + 1) * 128]
    u_t = jnp.concatenate([ut_ref[c] for c in range(n_slab)], axis=1)
    ub = u_t.astype(BF16)
    n_blk = S5_WIDTH // 128
    lanes_blk = S5_LANES // n_blk
    for k in range(n_blk):
        rs = slice(k * 128, (k + 1) * 128)
        ls = slice(k * lanes_blk, (k + 1) * lanes_blk)
        xr_ref[:, ls] = _dot(ub[:, rs], bre_ref[rs, ls])
        xi_ref[:, ls] = _dot(ub[:, rs], bim_ref[rs, ls])

    for c in range(S5_LANES // lane_chunk):
        cs = slice(c * lane_chunk, (c + 1) * lane_chunk)
        a_re = jnp.broadcast_to(ar_ref[:, cs], (nb, lane_chunk))
        a_im = jnp.broadcast_to(ai_ref[:, cs], (nb, lane_chunk))

        def step(t, carry, cs=cs, a_re=a_re, a_im=a_im):
            s_re, s_im = carry
            r0 = pl.multiple_of(t * nb, nb)
            n_re = a_re * s_re - a_im * s_im + xr_ref[pl.ds(r0, nb), cs]
            n_im = a_re * s_im + a_im * s_re + xi_ref[pl.ds(r0, nb), cs]
            xr_ref[pl.ds(r0, nb), cs] = n_re
            xi_ref[pl.ds(r0, nb), cs] = n_im
            return n_re, n_im

        s_re, s_im = lax.fori_loop(0, L, step, (sr_ref[:, cs], si_ref[:, cs]), unroll=2)
        sr_ref[:, cs] = s_re
        si_ref[:, cs] = s_im

    for k in range(n_blk):
        rs = slice(k * 128, (k + 1) * 128)
        ls = slice(k * lanes_blk, (k + 1) * lanes_blk)
        y = _dot(xr_ref[:, ls].astype(BF16), cre_ref[ls, rs]) - _dot(xi_ref[:, ls].astype(BF16), cim_ref[ls, rs])
        y_ref[k] = jax.nn.gelu(y + d_ref[:, rs] * u_t[:, rs])
    for b in range(nb):
        for c in range(n_slab):
            o_ref[b, :, c * 128:(c + 1) * 128] = y_ref[c, pl.ds(b, L, stride=nb), :].astype(BF16)


def s5_scan(zrow, b_re_full, b_im_full, c_re_full, c_im_full, a_re, a_im, d_skip, *, chunk=64, lane_chunk=1024):
    b, s, _ = zrow.shape
    rows = chunk * b
    return pl.pallas_call(
        functools.partial(_s5_scan_kernel, lane_chunk=lane_chunk),
        out_shape=jax.ShapeDtypeStruct((b, s, S5_WIDTH), BF16),
        grid=(s // chunk,),
        in_specs=[
            pl.BlockSpec((b, chunk, S5_WIDTH), lambda i: (0, i, COL_S5 // S5_WIDTH)),
            _resident((S5_WIDTH, S5_LANES)),
            _resident((S5_WIDTH, S5_LANES)),
            _resident((S5_LANES, S5_WIDTH)),
            _resident((S5_LANES, S5_WIDTH)),
            _resident((1, S5_LANES)),
            _resident((1, S5_LANES)),
            _resident((1, S5_WIDTH)),
        ],
        out_specs=pl.BlockSpec((b, chunk, S5_WIDTH), lambda i: (0, i, 0)),
        scratch_shapes=[
            pltpu.VMEM((S5_WIDTH // 128, rows, 128), F32),
            pltpu.VMEM((rows, S5_LANES), F32),
            pltpu.VMEM((rows, S5_LANES), F32),
            pltpu.VMEM((b, S5_LANES), F32),
            pltpu.VMEM((b, S5_LANES), F32),
            pltpu.VMEM((S5_WIDTH // 128, rows, 128), F32),
        ],
        compiler_params=_params("arbitrary"),
        name="s5_scan",
    )(zrow, b_re_full, b_im_full, c_re_full, c_im_full, a_re, a_im, d_skip)


def _rel_bias_kernel(rb_ref, o_ref):
    p = pl.program_id(0)
    ql = lax.broadcasted_iota(I32, (Q_BLOCK, 2 * Q_BLOCK), 0)
    c = lax.broadcasted_iota(I32, (Q_BLOCK, 2 * Q_BLOCK), 1)
    dist = jnp.maximum(ql - c + p * Q_BLOCK, 0)
    max_exact = REL_BUCKETS // 2
    d_f = jnp.maximum(dist, 1).astype(F32)
    large = max_exact + (jnp.log(d_f / max_exact) / math.log(REL_MAX_DISTANCE / max_exact)
                         * (REL_BUCKETS - max_exact)).astype(I32)
    large = jnp.minimum(large, REL_BUCKETS - 1)
    bucket = jnp.where(dist < max_exact, dist, large)
    for h in range(ATTN_HEADS):
        acc = jnp.zeros((Q_BLOCK, 2 * Q_BLOCK), F32)
        for k in range(REL_BUCKETS):
            acc = jnp.where(bucket == k, rb_ref[k, h], acc)
        o_ref[h] = acc - rb_ref[REL_BUCKETS - 1, h]


def rel_bias_tiles(rel_bias):
    return pl.pallas_call(
        _rel_bias_kernel,
        out_shape=jax.ShapeDtypeStruct((2, ATTN_HEADS, Q_BLOCK, 2 * Q_BLOCK), F32),
        grid=(2,),
        in_specs=[pl.BlockSpec(memory_space=pltpu.SMEM)],
        out_specs=pl.BlockSpec((None, ATTN_HEADS, Q_BLOCK, 2 * Q_BLOCK), lambda p: (p, 0, 0, 0)),
        name="rel_bias_tiles",
    )(rel_bias)


COUNT_CHAINS = 8


def _count(ref, pred, *, packed):
    rows = 16 if packed else 8
    one, zero = (jnp.int16(1), jnp.int16(0)) if packed else (1.0, 0.0)
    n = ref.shape[0] // rows
    chains = min(COUNT_CHAINS, n)
    accs = []
    for j in range(n):
        hit = jnp.where(pred(ref[j * rows:(j + 1) * rows, :]), one, zero)
        if j < chains:
            accs.append(hit)
        else:
            accs[j % chains] = accs[j % chains] + hit
    while len(accs) > 1:
        accs = [accs[c] + accs[c + len(accs) // 2] for c in range(len(accs) // 2)]
    return jnp.sum(accs[0].astype(F32), axis=0, keepdims=True)


KEY_NEG_INF = -2139095041
NO_TIE_POS = 2 ** 30
AUG_DEPTH = 256
KEY_CHUNK = 2048
I16 = jnp.int16
I16_MIN = -(2 ** 15)


def _dsa_kernel(q_ref, k_ref, v_ref, ki_ref, qit_ref, wit_ref, bias_ref, o_ref,
                kaug_ref, qaug_ref, key_ref, hi_ref, lo_ref, tie_ref, mask_ref, oh_ref, logit_ref, *, qb0, topk, hp):
    sc = v_ref.shape[0]
    rows = hp * Q_BLOCK
    i = pl.program_id(1) + qb0
    n0 = jnp.maximum(i - 1, 0)

    @pl.when(pl.program_id(1) == 0)
    def _():
        kaug_ref[:, Q_BLOCK:] = k_ref[...]
        r = lax.broadcasted_iota(I32, (rows, Q_BLOCK), 0)
        c = lax.broadcasted_iota(I32, (rows, Q_BLOCK), 1)
        eye = jnp.where(jnp.bitwise_and(r, Q_BLOCK - 1) == c, 1.0, 0.0).astype(BF16)
        for g in range(ATTN_HEADS // hp):
            qaug_ref[g, :, 0:Q_BLOCK] = eye
            qaug_ref[g, :, Q_BLOCK + ATTN_HEAD_DIM:] = jnp.zeros((rows, AUG_DEPTH - Q_BLOCK - ATTN_HEAD_DIM), BF16)

    kib = ki_ref[...]
    pad = jnp.zeros((128 - IDX_HEAD_DIM, 2 * Q_BLOCK), BF16)
    score = jnp.zeros((sc, Q_BLOCK), F32)
    for h in range(0, IDX_HEADS, 2):
        qh = jnp.concatenate(
            [qit_ref[h * IDX_HEAD_DIM:(h + 1) * IDX_HEAD_DIM, :], qit_ref[(h + 1) * IDX_HEAD_DIM:(h + 2) * IDX_HEAD_DIM, :]],
            axis=1).astype(BF16)
        rel = jnp.maximum(_dot(kib, jnp.concatenate([qh, pad], axis=0)), 0.0)
        score = score + rel[:, :Q_BLOCK] * wit_ref[h:h + 1, :] + rel[:, Q_BLOCK:] * wit_ref[h + 1:h + 2, :]
    score = score * ((IDX_HEADS ** -0.5) * (IDX_HEAD_DIM ** -0.5))
    kpos = lax.broadcasted_iota(I32, (sc, Q_BLOCK), 0)
    always_causal = qb0 * Q_BLOCK
    score = score + 0.0
    tail_shape = (sc - always_causal, Q_BLOCK)
    tail_kpos = lax.broadcasted_iota(I32, tail_shape, 0) + always_causal
    tail_qpos = lax.broadcasted_iota(I32, tail_shape, 1) + i * Q_BLOCK
    tail = jnp.where(tail_kpos <= tail_qpos, score[always_causal:], -jnp.inf)
    score = jnp.concatenate([score[:always_causal], tail], axis=0) if always_causal else tail
    bits = pltpu.bitcast(score, I32)
    key = jnp.where(bits < 0, bits ^ 0x7FFFFFFF, bits)
    key_ref[...] = key
    hi_ref[...] = jnp.right_shift(key, 16).astype(I16)

    def half_search(ref, k_needed):
        def step(t, thr):
            cand = thr + jnp.left_shift(jnp.int32(1), 15 - t)
            cand16 = cand.astype(I16)
            cnt = _count(ref, lambda keys: keys >= cand16, packed=True)
            return jnp.where(cnt >= k_needed, cand, thr)

        return lax.fori_loop(0, 16, step, jnp.full((1, Q_BLOCK), I16_MIN, I32))

    thr_hi = half_search(hi_ref, float(topk))
    thr_hi16 = thr_hi.astype(I16)
    above = _count(hi_ref, lambda keys: keys > thr_hi16, packed=True)
    lo = (jnp.bitwise_and(key_ref[...], 0xFFFF) + I16_MIN).astype(I16)
    lo_ref[...] = jnp.where(hi_ref[...] == thr_hi16, lo, jnp.int16(I16_MIN))
    thr_lo = half_search(lo_ref, float(topk) - above)
    thr = thr_hi * 65536 + (thr_lo - I16_MIN)

    n_gt = _count(key_ref, lambda keys: keys > thr, packed=False)
    n_ge = _count(key_ref, lambda keys: keys >= thr, packed=False)
    need = topk - n_gt
    tie_ref[...] = jnp.where(key_ref[...] == thr, kpos, jnp.int32(NO_TIE_POS))
    at_neg_inf = thr == KEY_NEG_INF
    excess = jnp.where(jnp.logical_and(n_ge > topk, jnp.logical_not(at_neg_inf)), 1, 0)

    def index_search():
        def index_step(t, m):
            cand = m + jnp.left_shift(jnp.int32(1), 10 - t)
            cnt = _count(tie_ref, lambda pos: pos < cand, packed=False)
            return jnp.where(cnt < need, cand, m)

        return lax.fori_loop(0, 11, index_step, jnp.zeros((1, Q_BLOCK), I32))

    m = lax.cond(jnp.max(excess) > 0, index_search, lambda: jnp.full((1, Q_BLOCK), NO_TIE_POS - 1, I32))
    m = jnp.where(at_neg_inf, -1, m)
    mask = jnp.where(key_ref[...] > thr, 0.0, jnp.where(tie_ref[...] <= m, 0.0, NEG_BIG))
    mask_ref[...] = mask.astype(BF16)
    always_far = max(qb0 - 1, 0) * Q_BLOCK
    if always_far:
        kaug_ref[0:always_far, 0:Q_BLOCK] = mask[:always_far].astype(BF16)
    rest_kpos = lax.broadcasted_iota(I32, (sc - always_far, Q_BLOCK), 0) + always_far
    kaug_ref[always_far:, 0:Q_BLOCK] = jnp.where(rest_kpos < n0 * Q_BLOCK, mask[always_far:], NEG_BIG).astype(BF16)

    q = q_ref[...] * (ATTN_HEAD_DIM ** -0.5)
    for h in range(ATTN_HEADS):
        r0 = (h % hp) * Q_BLOCK
        qaug_ref[h // hp, r0:r0 + Q_BLOCK, Q_BLOCK:Q_BLOCK + ATTN_HEAD_DIM] = (
            q[:, h * ATTN_HEAD_DIM:(h + 1) * ATTN_HEAD_DIM].astype(BF16))

    near0 = pl.multiple_of(n0 * Q_BLOCK, Q_BLOCK)
    kaug_near = jnp.concatenate(
        [mask_ref[pl.ds(near0, 2 * Q_BLOCK), :], kaug_ref[pl.ds(near0, 2 * Q_BLOCK), Q_BLOCK:]], axis=1)
    v_near = v_ref[pl.ds(near0, 2 * Q_BLOCK), :]
    pidx = jnp.minimum(i, 1)

    far = sc - 2 * Q_BLOCK
    near_chunks = [(True, 0, 2 * Q_BLOCK)] if far > 0 else [(True, 0, Q_BLOCK), (True, Q_BLOCK, 2 * Q_BLOCK)]
    chunks = [(False, c0, min(c0 + KEY_CHUNK, far)) for c0 in range(0, far, KEY_CHUNK)] + near_chunks
    n_groups = ATTN_HEADS // hp

    def logits_chunk(g, chunk):
        is_near, a, b = chunk
        if is_near:
            l = _dot_nt(qaug_ref[g], kaug_near[a:b, :]) + bias_ref[pidx, g, :, a:b]
            logit_ref[g % 2, :, sc + a:sc + b] = l
        else:
            l = _dot_nt(qaug_ref[g], kaug_ref[a:b, :])
            logit_ref[g % 2, :, a:b] = l
        return jnp.max(l, axis=1, keepdims=True)

    def values_chunk(g, chunk, mx):
        is_near, a, b = chunk
        if is_near:
            p = jnp.exp((logit_ref[g % 2, :, sc + a:sc + b] - mx).astype(BF16))
            return _dot(p, v_near[a:b, :])
        p = jnp.exp((logit_ref[g % 2, :, a:b] - mx).astype(BF16))
        return _dot(p, v_ref[a:b, :])

    def row_max(parts):
        while len(parts) > 1:
            parts = [jnp.maximum(parts[k], parts[k + 1]) if k + 1 < len(parts) else parts[k]
                     for k in range(0, len(parts), 2)]
        return parts[0]

    mx = row_max([logits_chunk(0, chunk) for chunk in chunks])
    for g in range(n_groups):
        acc = None
        next_parts = []
        for chunk in chunks:
            if g + 1 < n_groups:
                next_parts.append(logits_chunk(g + 1, chunk))
            part = values_chunk(g, chunk, mx)
            acc = part if acc is None else acc + part
        den = acc[:, ATTN_HEAD_DIM:ATTN_HEAD_DIM + 1]
        oh_ref[g] = acc[:, 0:ATTN_HEAD_DIM] / den
        if g + 1 < n_groups:
            mx = row_max(next_parts)
    o_ref[...] = jnp.concatenate(
        [oh_ref[h // hp, (h % hp) * Q_BLOCK:(h % hp + 1) * Q_BLOCK, :] for h in range(ATTN_HEADS)],
        axis=1).astype(BF16)


def dsa_attention(zrow, zkv, idx_t, bias_tiles, *, qb0, qb1, hp=2):
    b, s, _ = zrow.shape
    sc = qb1 * Q_BLOCK
    assert sc <= 2048, "the tie-break index search covers 11 bits"
    nq = qb1 - qb0
    nqs = s // Q_BLOCK
    ng = ATTN_HEADS // hp
    rows = hp * Q_BLOCK
    topk = min(TOPK_MAX, s // TOPK_DIVISOR)
    kv_spec = lambda slab: pl.BlockSpec((None, sc, 128), lambda bi, qi: (bi, 0, slab))
    return pl.pallas_call(
        functools.partial(_dsa_kernel, qb0=qb0, topk=topk, hp=hp),
        out_shape=jax.ShapeDtypeStruct((b, nq * Q_BLOCK, ATTN_WIDTH), BF16),
        grid=(b, nq),
        in_specs=[
            pl.BlockSpec((None, Q_BLOCK, ATTN_WIDTH), lambda bi, qi: (bi, qi + qb0, COL_Q // ATTN_WIDTH)),
            kv_spec(KV_SLAB_K),
            kv_spec(KV_SLAB_V),
            kv_spec(KV_SLAB_KI),
            pl.BlockSpec((IDX_HEADS * IDX_HEAD_DIM, Q_BLOCK), lambda bi, qi: (0, bi * nqs + qi + qb0)),
            pl.BlockSpec((16, Q_BLOCK), lambda bi, qi: (IDX_HEADS * IDX_HEAD_DIM // 16, bi * nqs + qi + qb0)),
            _resident((2, ng, rows, 2 * Q_BLOCK)),
        ],
        out_specs=pl.BlockSpec((None, Q_BLOCK, ATTN_WIDTH), lambda bi, qi: (bi, qi, 0)),
        scratch_shapes=[
            pltpu.VMEM((sc, AUG_DEPTH), BF16),
            pltpu.VMEM((ng, rows, AUG_DEPTH), BF16),
            pltpu.VMEM((sc, Q_BLOCK), I32),
            pltpu.VMEM((sc, Q_BLOCK), I16),
            pltpu.VMEM((sc, Q_BLOCK), I16),
            pltpu.VMEM((sc, Q_BLOCK), I32),
            pltpu.VMEM((sc, Q_BLOCK), BF16),
            pltpu.VMEM((ng, rows, ATTN_HEAD_DIM), F32),
            pltpu.VMEM((2, rows, sc + 2 * Q_BLOCK), F32),
        ],
        compiler_params=_params("parallel", "arbitrary"),
        name="dsa_attention",
    )(zrow, zkv, zkv, zkv, idx_t, idx_t, bias_tiles.reshape(2, ng, rows, 2 * Q_BLOCK))


def _merge_kernel(x_ref, gpre_ref, wg_ref, pool_ref, wpool_ref, attn_ref, wattn_ref, s5_ref, wglu_ref,
                  wout_ref, gpost_ref, o_ref):
    x = x_ref[...]
    h = _rms(x, gpre_ref[...]).astype(BF16)
    gates = jax.nn.sigmoid(_dot(h, wg_ref[...]))
    y_pool = _dot(pool_ref[...], wpool_ref[...])
    y_attn = _dot(attn_ref[...], wattn_ref[...])
    glu = _dot(s5_ref[...], wglu_ref[...])
    y_s5 = glu[:, :D_MODEL] * jax.nn.sigmoid(glu[:, D_MODEL:])
    merged = (gates[:, 0:D_MODEL] * y_pool + gates[:, D_MODEL:2 * D_MODEL] * y_attn
              + gates[:, 2 * D_MODEL:3 * D_MODEL] * y_s5)
    mixed = _dot(merged.astype(BF16), wout_ref[...])
    o_ref[...] = x + _rms(mixed, gpost_ref[...])


def merge(x, g_pre, w_gates, pool_y, w_pool, attn_y, w_attn, s5_y, w_glu, w_out, g_post, *, tm=256):
    t, d = x.shape
    row = lambda width: pl.BlockSpec((tm, width), lambda i: (i, 0))
    full = lambda a: _resident(a.shape)
    return pl.pallas_call(
        _merge_kernel,
        out_shape=jax.ShapeDtypeStruct((t, d), F32),
        grid=(t // tm,),
        in_specs=[row(d), full(g_pre), full(w_gates), row(POOL_WIDTH), full(w_pool), row(ATTN_WIDTH),
                  full(w_attn), row(S5_WIDTH), full(w_glu), full(w_out), full(g_post)],
        out_specs=row(d),
        compiler_params=_params("parallel"),
        name="merge",
    )(x, g_pre, w_gates, pool_y, w_pool, attn_y, w_attn, s5_y, w_glu, w_out, g_post)


def _ffn_kernel(x_ref, gpre_ref, win_ref, wdown_ref, gpost_ref, o_ref, *, th):
    x = x_ref[...]
    h = _rms(x, gpre_ref[...]).astype(BF16)
    hid = wdown_ref.shape[0]
    f = None
    for c0 in range(0, hid, th):
        gate = _dot(h, win_ref[:, c0:c0 + th])
        up = _dot(h, win_ref[:, hid + c0:hid + c0 + th])
        act = (jax.nn.silu(gate) * up).astype(BF16)
        part = _dot(act, wdown_ref[c0:c0 + th, :])
        f = part if f is None else f + part
    o_ref[...] = x + _rms(f, gpost_ref[...])


def ffn(x, g_pre, w_in, w_down, g_post, *, tm=512, th=256):
    t, d = x.shape
    return pl.pallas_call(
        functools.partial(_ffn_kernel, th=th),
        out_shape=jax.ShapeDtypeStruct((t, d), F32),
        grid=(t // tm,),
        in_specs=[
            pl.BlockSpec((tm, d), lambda i: (i, 0)),
            _resident((1, d)),
            _resident(w_in.shape),
            _resident(w_down.shape),
            _resident((1, d)),
        ],
        out_specs=pl.BlockSpec((tm, d), lambda i: (i, 0)),
        compiler_params=_params("parallel"),
        name="ffn",
    )(x, g_pre, w_in, w_down, g_post)


def _pad_cols(w, width):
    return jnp.pad(w, ((0, 0), (0, width - w.shape[1])))


def _split_w_in(w):
    sizes = (POOL_WIDTH, ATTN_WIDTH, ATTN_HEAD_DIM, ATTN_HEAD_DIM, IDX_HEADS * IDX_HEAD_DIM, IDX_HEAD_DIM,
             IDX_HEADS, S5_WIDTH, N_BRANCHES * D_MODEL)
    parts, start = [], 0
    for size in sizes:
        parts.append(w[:, start:start + size])
        start += size
    w_pool, w_q, w_k, w_v, w_qi, w_ki, w_wi, w_s5, w_gates = parts
    w_rows = jnp.concatenate(
        [w_pool, w_s5, w_q, _pad_cols(w_k, 128), _pad_cols(w_v, 128), _pad_cols(w_ki, 128)], axis=1)
    w_cols = jnp.concatenate([w_qi, _pad_cols(w_wi, IDX_T_ROWS - IDX_HEADS * IDX_HEAD_DIM)], axis=1).T
    return w_rows.astype(BF16), w_cols.astype(BF16), w_gates.astype(BF16)


def _block_diag(blocks):
    g, r, c = blocks.shape
    eye = jnp.eye(g, dtype=blocks.dtype)
    return (blocks[:, :, None, :] * eye[:, None, :, None]).reshape(g * r, g * c)


def _attn_classes(nqs):
    n_classes = 8 if nqs % 8 == 0 and nqs >= 16 else 1
    step = nqs // n_classes
    return [(c * step, (c + 1) * step) for c in range(n_classes)]


def kernel(x, norm_mix_pre, norm_mix_post, norm_ffn_pre, norm_ffn_post, w_in, pool_mix_w, pool_scale, pool_out_w, attn_out_w, rel_bias, s5_lambda_re, s5_lambda_im, s5_log_dt, s5_b_re, s5_b_im, s5_c_re, s5_c_im, s5_d, s5_glu_w, w_out, ffn_w_in, ffn_w_out):
    b, s, d = x.shape
    t = b * s
    depth = w_in.shape[0]
    bias_tiles = rel_bias_tiles(rel_bias)
    xf = x.reshape(t, d)
    for l in range(depth):
        w_rows, w_cols, w_gates = _split_w_in(w_in[l])
        g_pre = norm_mix_pre[l][None, :]
        zrow, zkv, idx_t = in_proj(xf, g_pre, w_rows, w_cols)
        zrow = zrow.reshape(b, s, COL_K)
        zkv = zkv.reshape(b, s, ROW_WIDTH - COL_K)

        pool_y = pool_mixer(zrow, pool_mix_w[l].astype(BF16), pool_scale[l][None, :])

        a_re, a_im, bb_re, bb_im = s5_discretise(
            s5_lambda_re[l][:, None, :], s5_lambda_im[l][:, None, :], s5_log_dt[l][:, None, None],
            jnp.swapaxes(s5_b_re[l], 1, 2), jnp.swapaxes(s5_b_im[l], 1, 2))
        s5_y = s5_scan(
            zrow,
            _block_diag(bb_re).astype(BF16), _block_diag(bb_im).astype(BF16),
            _block_diag(jnp.swapaxes(s5_c_re[l], 1, 2)).astype(BF16),
            _block_diag(jnp.swapaxes(s5_c_im[l], 1, 2)).astype(BF16),
            a_re.reshape(1, S5_LANES), a_im.reshape(1, S5_LANES), s5_d[l][None, :])

        attn_y = jnp.concatenate(
            [dsa_attention(zrow, zkv, idx_t, bias_tiles, qb0=q0, qb1=q1)
             for q0, q1 in _attn_classes(s // Q_BLOCK)], axis=1)

        xf = merge(xf, g_pre, w_gates, pool_y.reshape(t, POOL_WIDTH), pool_out_w[l].astype(BF16),
                   attn_y.reshape(t, ATTN_WIDTH), attn_out_w[l].astype(BF16),
                   s5_y.reshape(t, S5_WIDTH), s5_glu_w[l].astype(BF16), w_out[l].astype(BF16),
                   norm_mix_post[l][None, :])
        xf = ffn(xf, norm_ffn_pre[l][None, :], ffn_w_in[l].astype(BF16), ffn_w_out[l].astype(BF16),
                 norm_ffn_post[l][None, :])
    return xf.reshape(b, s, d)
```

```python
import functools
import math

import jax
import jax.numpy as jnp
from jax import lax
from jax.experimental import pallas as pl
from jax.experimental.pallas import tpu as pltpu

F32 = jnp.float32
BF16 = jnp.bfloat16
I32 = jnp.int32

D_MODEL = 1024
POOL_WINDOWS = (2, 4, 8, 16)
POOL_WIDTH = 512
POOL_GROUP = 128
POOL_HALO = 16
ATTN_HEADS = 16
ATTN_HEAD_DIM = 64
ATTN_WIDTH = 1024
IDX_HEADS = 8
IDX_HEAD_DIM = 64
TOPK_MAX = 256
TOPK_DIVISOR = 4
Q_BLOCK = 128
REL_BUCKETS = 32
REL_MAX_DISTANCE = 128
S5_WIDTH = 512
S5_GROUP = 16
S5_GROUPS = 32
S5_STATE = 64
S5_LANES = S5_GROUPS * S5_STATE
N_BRANCHES = 3
FFN_HIDDEN = 2816
NORM_EPS = 1e-6

COL_POOL = 0
COL_S5 = 512
COL_Q = 1024
COL_K = 2048
KV_SLAB_K, KV_SLAB_V, KV_SLAB_KI = 0, 1, 2
ROW_WIDTH = COL_K + 3 * 128
IDX_T_ROWS = 528

NEG_BIG = -1e30
LOG2_E = math.log2(math.e)
INT_MIN = -(2 ** 31)

VMEM_LIMIT = 56 * 1024 * 1024


def _rms(x, g):
    ms = jnp.mean(x * x, axis=-1, keepdims=True)
    return x * lax.rsqrt(ms + NORM_EPS) * g


def _dot(a, b):
    return jnp.dot(a, b, preferred_element_type=F32)


def _dot_nt(a, b):
    return lax.dot_general(a, b, (((1,), (1,)), ((), ())), preferred_element_type=F32)


def _params(*sem):
    return pltpu.CompilerParams(dimension_semantics=sem, vmem_limit_bytes=VMEM_LIMIT)


def _resident(shape):
    return pl.BlockSpec(shape, lambda *_: (0,) * len(shape), pipeline_mode=pl.Buffered(1))


def _in_proj_kernel(x_ref, g_ref, w_ref, wt_ref, rows_ref, kv_ref, cols_ref):
    h = _rms(x_ref[...], g_ref[...]).astype(BF16)
    z = _dot(h, w_ref[...])
    rows_ref[...] = z[:, :COL_K]
    kv = z[:, COL_K:]
    lane = lax.broadcasted_iota(I32, kv.shape, 1)
    kv_ref[...] = jnp.where(lane == KV_SLAB_V * 128 + ATTN_HEAD_DIM, 1.0, kv).astype(BF16)
    cols_ref[...] = _dot_nt(wt_ref[...], h)


def in_proj(x, gain, w, wt, *, tm=512):
    t, d = x.shape
    n = w.shape[1]
    c = wt.shape[0]
    return pl.pallas_call(
        _in_proj_kernel,
        out_shape=(jax.ShapeDtypeStruct((t, COL_K), F32), jax.ShapeDtypeStruct((t, n - COL_K), BF16),
                   jax.ShapeDtypeStruct((c, t), F32)),
        grid=(t // tm,),
        in_specs=[
            pl.BlockSpec((tm, d), lambda i: (i, 0)),
            _resident((1, d)),
            _resident((d, n)),
            _resident((c, d)),
        ],
        out_specs=(pl.BlockSpec((tm, COL_K), lambda i: (i, 0)), pl.BlockSpec((tm, n - COL_K), lambda i: (i, 0)),
                   pl.BlockSpec((c, tm), lambda i: (0, i))),
        compiler_params=_params("parallel"),
        name="in_proj",
    )(x, gain, w, wt)


def _pool_kernel(u_ref, mixw_ref, scale_ref, o_ref, buf_ref):
    s = u_ref.shape[0]
    buf_ref[0:POOL_HALO, :] = jnp.zeros((POOL_HALO, POOL_GROUP), F32)
    pos = lax.broadcasted_iota(I32, (s, POOL_GROUP), 0).astype(F32)
    for gi, w in enumerate(POOL_WINDOWS):
        assert w & (w - 1) == 0 and w <= POOL_HALO
        c0 = gi * POOL_GROUP
        tok = u_ref[:, c0:c0 + POOL_GROUP]
        acc = tok
        shift = 1
        while shift < w:
            buf_ref[POOL_HALO:POOL_HALO + s, :] = acc
            acc = acc + buf_ref[POOL_HALO - shift:POOL_HALO - shift + s, :]
            shift *= 2
        count = jnp.minimum(pos + 1.0, float(w))
        d = acc / count - tok
        y = _dot(d.astype(BF16), mixw_ref[gi])
        o_ref[:, c0:c0 + POOL_GROUP] = (y * scale_ref[:, c0:c0 + POOL_GROUP]).astype(BF16)


def pool_mixer(zrow, mix_w, scale):
    b, s, _ = zrow.shape
    return pl.pallas_call(
        _pool_kernel,
        out_shape=jax.ShapeDtypeStruct((b, s, POOL_WIDTH), BF16),
        grid=(b,),
        in_specs=[
            pl.BlockSpec((None, s, POOL_WIDTH), lambda i: (i, 0, COL_POOL // POOL_WIDTH)),
            pl.BlockSpec((len(POOL_WINDOWS), POOL_GROUP, POOL_GROUP), lambda i: (0, 0, 0)),
            pl.BlockSpec((1, POOL_WIDTH), lambda i: (0, 0)),
        ],
        out_specs=pl.BlockSpec((None, s, POOL_WIDTH), lambda i: (i, 0, 0)),
        scratch_shapes=[pltpu.VMEM((POOL_HALO + s, POOL_GROUP), F32)],
        compiler_params=_params("parallel"),
        name="pool_mixer",
    )(zrow, mix_w, scale)


def _s5_disc_kernel(lr_ref, li_ref, ldt_ref, br_ref, bi_ref, ar_ref, ai_ref, bbr_ref, bbi_ref):
    lr = lr_ref[...]
    li = li_ref[...]
    dt = jnp.exp(ldt_ref[...])
    mag = jnp.exp(lr * dt)
    a_re = mag * jnp.cos(li * dt)
    a_im = mag * jnp.sin(li * dt)
    den = lr * lr + li * li
    coef_re = ((a_re - 1.0) * lr + a_im * li) / den
    coef_im = (a_im * lr - (a_re - 1.0) * li) / den
    ar_ref[...] = a_re
    ai_ref[...] = a_im
    br = br_ref[...]
    bi = bi_ref[...]
    bbr_ref[...] = coef_re * br - coef_im * bi
    bbi_ref[...] = coef_re * bi + coef_im * br


def s5_discretise(lam_re, lam_im, log_dt, b_re_t, b_im_t):
    g, _, n = lam_re.shape
    j = b_re_t.shape[1]
    return pl.pallas_call(
        _s5_disc_kernel,
        out_shape=(
            jax.ShapeDtypeStruct((g, 1, n), F32),
            jax.ShapeDtypeStruct((g, 1, n), F32),
            jax.ShapeDtypeStruct((g, j, n), F32),
            jax.ShapeDtypeStruct((g, j, n), F32),
        ),
        name="s5_discretise",
    )(lam_re, lam_im, log_dt, b_re_t, b_im_t)


def _s5_scan_kernel(u_ref, bre_ref, bim_ref, cre_ref, cim_ref, ar_ref, ai_ref, d_ref, o_ref,
                    ut_ref, xr_ref, xi_ref, sr_ref, si_ref, y_ref, *, lane_chunk):
    nb, L, _ = u_ref.shape

    @pl.when(pl.program_id(0) == 0)
    def _():
        sr_ref[...] = jnp.zeros_like(sr_ref)
        si_ref[...] = jnp.zeros_like(si_ref)

    n_slab = S5_WIDTH // 128
    for b in range(nb):
        for c in range(n_slab):
            ut_ref[c, pl.ds(b, L, stride=nb), :] = u_ref[b, :, c * 128:(c + 1) * 128]
    u_t = jnp.concatenate([ut_ref[c] for c in range(n_slab)], axis=1)
    ub = u_t.astype(BF16)
    n_blk = S5_WIDTH // 128
    lanes_blk = S5_LANES // n_blk
    for k in range(n_blk):
        rs = slice(k * 128, (k + 1) * 128)
        ls = slice(k * lanes_blk, (k + 1) * lanes_blk)
        xr_ref[:, ls] = _dot(ub[:, rs], bre_ref[rs, ls])
        xi_ref[:, ls] = _dot(ub[:, rs], bim_ref[rs, ls])

    for c in range(S5_LANES // lane_chunk):
        cs = slice(c * lane_chunk, (c + 1) * lane_chunk)
        a_re = jnp.broadcast_to(ar_ref[:, cs], (nb, lane_chunk))
        a_im = jnp.broadcast_to(ai_ref[:, cs], (nb, lane_chunk))

        def step(t, carry, cs=cs, a_re=a_re, a_im=a_im):
            s_re, s_im = carry
            r0 = pl.multiple_of(t * nb, nb)
            n_re = a_re * s_re - a_im * s_im + xr_ref[pl.ds(r0, nb), cs]
            n_im = a_re * s_im + a_im * s_re + xi_ref[pl.ds(r0, nb), cs]
            xr_ref[pl.ds(r0, nb), cs] = n_re
            xi_ref[pl.ds(r0, nb), cs] = n_im
            return n_re, n_im

        s_re, s_im = lax.fori_loop(0, L, step, (sr_ref[:, cs], si_ref[:, cs]), unroll=2)
        sr_ref[:, cs] = s_re
        si_ref[:, cs] = s_im

    for k in range(n_blk):
        rs = slice(k * 128, (k + 1) * 128)
        ls = slice(k * lanes_blk, (k + 1) * lanes_blk)
        y = _dot(xr_ref[:, ls].astype(BF16), cre_ref[ls, rs]) - _dot(xi_ref[:, ls].astype(BF16), cim_ref[ls, rs])
        y_ref[k] = jax.nn.gelu(y + d_ref[:, rs] * u_t[:, rs])
    for b in range(nb):
        for c in range(n_slab):
            o_ref[b, :, c * 128:(c + 1) * 128] = y_ref[c, pl.ds(b, L, stride=nb), :].astype(BF16)


def s5_scan(zrow, b_re_full, b_im_full, c_re_full, c_im_full, a_re, a_im, d_skip, *, chunk=64, lane_chunk=1024):
    b, s, _ = zrow.shape
    rows = chunk * b
    return pl.pallas_call(
        functools.partial(_s5_scan_kernel, lane_chunk=lane_chunk),
        out_shape=jax.ShapeDtypeStruct((b, s, S5_WIDTH), BF16),
        grid=(s // chunk,),
        in_specs=[
            pl.BlockSpec((b, chunk, S5_WIDTH), lambda i: (0, i, COL_S5 // S5_WIDTH)),
            _resident((S5_WIDTH, S5_LANES)),
            _resident((S5_WIDTH, S5_LANES)),
            _resident((S5_LANES, S5_WIDTH)),
            _resident((S5_LANES, S5_WIDTH)),
            _resident((1, S5_LANES)),
            _resident((1, S5_LANES)),
            _resident((1, S5_WIDTH)),
        ],
        out_specs=pl.BlockSpec((b, chunk, S5_WIDTH), lambda i: (0, i, 0)),
        scratch_shapes=[
            pltpu.VMEM((S5_WIDTH // 128, rows, 128), F32),
            pltpu.VMEM((rows, S5_LANES), F32),
            pltpu.VMEM((rows, S5_LANES), F32),
            pltpu.VMEM((b, S5_LANES), F32),
            pltpu.VMEM((b, S5_LANES), F32),
            pltpu.VMEM((S5_WIDTH // 128, rows, 128), F32),
        ],
        compiler_params=_params("arbitrary"),
        name="s5_scan",
    )(zrow, b_re_full, b_im_full, c_re_full, c_im_full, a_re, a_im, d_skip)


def _rel_bias_kernel(rb_ref, o_ref):
    p = pl.program_id(0)
    ql = lax.broadcasted_iota(I32, (Q_BLOCK, 2 * Q_BLOCK), 0)
    c = lax.broadcasted_iota(I32, (Q_BLOCK, 2 * Q_BLOCK), 1)
    dist = jnp.maximum(ql - c + p * Q_BLOCK, 0)
    max_exact = REL_BUCKETS // 2
    d_f = jnp.maximum(dist, 1).astype(F32)
    large = max_exact + (jnp.log(d_f / max_exact) / math.log(REL_MAX_DISTANCE / max_exact)
                         * (REL_BUCKETS - max_exact)).astype(I32)
    large = jnp.minimum(large, REL_BUCKETS - 1)
    bucket = jnp.where(dist < max_exact, dist, large)
    for h in range(ATTN_HEADS):
        acc = jnp.zeros((Q_BLOCK, 2 * Q_BLOCK), F32)
        for k in range(REL_BUCKETS):
            acc = jnp.where(bucket == k, rb_ref[k, h], acc)
        o_ref[h] = (acc - rb_ref[REL_BUCKETS - 1, h]) * LOG2_E


def rel_bias_tiles(rel_bias):
    return pl.pallas_call(
        _rel_bias_kernel,
        out_shape=jax.ShapeDtypeStruct((2, ATTN_HEADS, Q_BLOCK, 2 * Q_BLOCK), F32),
        grid=(2,),
        in_specs=[pl.BlockSpec(memory_space=pltpu.SMEM)],
        out_specs=pl.BlockSpec((None, ATTN_HEADS, Q_BLOCK, 2 * Q_BLOCK), lambda p: (p, 0, 0, 0)),
        name="rel_bias_tiles",
    )(rel_bias)


COUNT_CHAINS = 8


def _count(ref, pred, *, packed):
    rows = 16 if packed else 8
    one, zero = (jnp.int16(1), jnp.int16(0)) if packed else (1.0, 0.0)
    n = ref.shape[0] // rows
    chains = min(COUNT_CHAINS, n)
    accs = []
    for j in range(n):
        hit = jnp.where(pred(ref[j * rows:(j + 1) * rows, :]), one, zero)
        if j < chains:
            accs.append(hit)
        else:
            accs[j % chains] = accs[j % chains] + hit
    while len(accs) > 1:
        accs = [accs[c] + accs[c + len(accs) // 2] for c in range(len(accs) // 2)]
    return jnp.sum(accs[0].astype(F32), axis=0, keepdims=True)


KEY_NEG_INF = -2139095041
NO_TIE_POS = 2 ** 30
AUG_DEPTH = 256
KEY_CHUNK = 2048
I16 = jnp.int16
I16_MIN = -(2 ** 15)


def _dsa_kernel(q_ref, k_ref, v_ref, ki_ref, qit_ref, wit_ref, bias_ref, o_ref,
                kaug_ref, qaug_ref, key_ref, hi_ref, lo_ref, tie_ref, mask_ref, oh_ref, logit_ref, *, qb0, topk, hp):
    sc = v_ref.shape[0]
    rows = hp * Q_BLOCK
    i = pl.program_id(1) + qb0
    n0 = jnp.maximum(i - 1, 0)

    @pl.when(pl.program_id(1) == 0)
    def _():
        kaug_ref[:, Q_BLOCK:] = k_ref[...]
        r = lax.broadcasted_iota(I32, (rows, Q_BLOCK), 0)
        c = lax.broadcasted_iota(I32, (rows, Q_BLOCK), 1)
        eye = jnp.where(jnp.bitwise_and(r, Q_BLOCK - 1) == c, 1.0, 0.0).astype(BF16)
        for g in range(ATTN_HEADS // hp):
            qaug_ref[g, :, 0:Q_BLOCK] = eye
            qaug_ref[g, :, Q_BLOCK + ATTN_HEAD_DIM:] = jnp.zeros((rows, AUG_DEPTH - Q_BLOCK - ATTN_HEAD_DIM), BF16)

    kib = ki_ref[...]
    pad = jnp.zeros((128 - IDX_HEAD_DIM, 2 * Q_BLOCK), BF16)
    score = jnp.zeros((sc, Q_BLOCK), F32)
    for h in range(0, IDX_HEADS, 2):
        qh = jnp.concatenate(
            [qit_ref[h * IDX_HEAD_DIM:(h + 1) * IDX_HEAD_DIM, :], qit_ref[(h + 1) * IDX_HEAD_DIM:(h + 2) * IDX_HEAD_DIM, :]],
            axis=1).astype(BF16)
        rel = jnp.maximum(_dot(kib, jnp.concatenate([qh, pad], axis=0)), 0.0)
        score = score + rel[:, :Q_BLOCK] * wit_ref[h:h + 1, :] + rel[:, Q_BLOCK:] * wit_ref[h + 1:h + 2, :]
    score = score * ((IDX_HEADS ** -0.5) * (IDX_HEAD_DIM ** -0.5))
    kpos = lax.broadcasted_iota(I32, (sc, Q_BLOCK), 0)
    always_causal = qb0 * Q_BLOCK
    score = score + 0.0
    tail_shape = (sc - always_causal, Q_BLOCK)
    tail_kpos = lax.broadcasted_iota(I32, tail_shape, 0) + always_causal
    tail_qpos = lax.broadcasted_iota(I32, tail_shape, 1) + i * Q_BLOCK
    tail = jnp.where(tail_kpos <= tail_qpos, score[always_causal:], -jnp.inf)
    score = jnp.concatenate([score[:always_causal], tail], axis=0) if always_causal else tail
    bits = pltpu.bitcast(score, I32)
    key = jnp.where(bits < 0, bits ^ 0x7FFFFFFF, bits)
    key_ref[...] = key
    hi_ref[...] = jnp.right_shift(key, 16).astype(I16)

    def half_search(ref, k_needed):
        def step(t, thr):
            cand = thr + jnp.left_shift(jnp.int32(1), 15 - t)
            cand16 = cand.astype(I16)
            cnt = _count(ref, lambda keys: keys >= cand16, packed=True)
            return jnp.where(cnt >= k_needed, cand, thr)

        return lax.fori_loop(0, 16, step, jnp.full((1, Q_BLOCK), I16_MIN, I32))

    thr_hi = half_search(hi_ref, float(topk))
    thr_hi16 = thr_hi.astype(I16)
    above = _count(hi_ref, lambda keys: keys > thr_hi16, packed=True)
    lo = (jnp.bitwise_and(key_ref[...], 0xFFFF) + I16_MIN).astype(I16)
    lo_ref[...] = jnp.where(hi_ref[...] == thr_hi16, lo, jnp.int16(I16_MIN))
    thr_lo = half_search(lo_ref, float(topk) - above)
    thr = thr_hi * 65536 + (thr_lo - I16_MIN)

    n_gt = _count(key_ref, lambda keys: keys > thr, packed=False)
    n_ge = _count(key_ref, lambda keys: keys >= thr, packed=False)
    need = topk - n_gt
    tie_ref[...] = jnp.where(key_ref[...] == thr, kpos, jnp.int32(NO_TIE_POS))
    at_neg_inf = thr == KEY_NEG_INF
    excess = jnp.where(jnp.logical_and(n_ge > topk, jnp.logical_not(at_neg_inf)), 1, 0)

    def index_search():
        def index_step(t, m):
            cand = m + jnp.left_shift(jnp.int32(1), 10 - t)
            cnt = _count(tie_ref, lambda pos: pos < cand, packed=False)
            return jnp.where(cnt < need, cand, m)

        return lax.fori_loop(0, 11, index_step, jnp.zeros((1, Q_BLOCK), I32))

    m = lax.cond(jnp.max(excess) > 0, index_search, lambda: jnp.full((1, Q_BLOCK), NO_TIE_POS - 1, I32))
    m = jnp.where(at_neg_inf, -1, m)
    mask = jnp.where(key_ref[...] > thr, 0.0, jnp.where(tie_ref[...] <= m, 0.0, NEG_BIG))
    mask_ref[...] = mask.astype(BF16)
    always_far = max(qb0 - 1, 0) * Q_BLOCK
    if always_far:
        kaug_ref[0:always_far, 0:Q_BLOCK] = mask[:always_far].astype(BF16)
    rest_kpos = lax.broadcasted_iota(I32, (sc - always_far, Q_BLOCK), 0) + always_far
    kaug_ref[always_far:, 0:Q_BLOCK] = jnp.where(rest_kpos < n0 * Q_BLOCK, mask[always_far:], NEG_BIG).astype(BF16)

    q = q_ref[...] * (ATTN_HEAD_DIM ** -0.5 * LOG2_E)
    for h in range(ATTN_HEADS):
        r0 = (h % hp) * Q_BLOCK
        qaug_ref[h // hp, r0:r0 + Q_BLOCK, Q_BLOCK:Q_BLOCK + ATTN_HEAD_DIM] = (
            q[:, h * ATTN_HEAD_DIM:(h + 1) * ATTN_HEAD_DIM].astype(BF16))

    near0 = pl.multiple_of(n0 * Q_BLOCK, Q_BLOCK)
    kaug_near = jnp.concatenate(
        [mask_ref[pl.ds(near0, 2 * Q_BLOCK), :], kaug_ref[pl.ds(near0, 2 * Q_BLOCK), Q_BLOCK:]], axis=1)
    v_near = v_ref[pl.ds(near0, 2 * Q_BLOCK), :]
    pidx = jnp.minimum(i, 1)

    far = sc - 2 * Q_BLOCK
    near_chunks = [(True, 0, 2 * Q_BLOCK)] if far > 0 else [(True, 0, Q_BLOCK), (True, Q_BLOCK, 2 * Q_BLOCK)]
    chunks = [(False, c0, min(c0 + KEY_CHUNK, far)) for c0 in range(0, far, KEY_CHUNK)] + near_chunks
    n_groups = ATTN_HEADS // hp

    def logits_chunk(g, chunk):
        is_near, a, b = chunk
        if is_near:
            l = _dot_nt(qaug_ref[g], kaug_near[a:b, :]) + bias_ref[pidx, g, :, a:b]
            logit_ref[g % 2, :, sc + a:sc + b] = l
        else:
            l = _dot_nt(qaug_ref[g], kaug_ref[a:b, :])
            logit_ref[g % 2, :, a:b] = l
        return jnp.max(l, axis=1, keepdims=True)

    def values_chunk(g, chunk, mx):
        is_near, a, b = chunk
        if is_near:
            p = jnp.exp2((logit_ref[g % 2, :, sc + a:sc + b] - mx).astype(BF16))
            return _dot(p, v_near[a:b, :])
        p = jnp.exp2((logit_ref[g % 2, :, a:b] - mx).astype(BF16))
        return _dot(p, v_ref[a:b, :])

    def row_max(parts):
        while len(parts) > 1:
            parts = [jnp.maximum(parts[k], parts[k + 1]) if k + 1 < len(parts) else parts[k]
                     for k in range(0, len(parts), 2)]
        return parts[0]

    mx = row_max([logits_chunk(0, chunk) for chunk in chunks])
    for g in range(n_groups):
        acc = None
        next_parts = []
        for chunk in chunks:
            if g + 1 < n_groups:
                next_parts.append(logits_chunk(g + 1, chunk))
            part = values_chunk(g, chunk, mx)
            acc = part if acc is None else acc + part
        den = acc[:, ATTN_HEAD_DIM:ATTN_HEAD_DIM + 1]
        oh_ref[g] = acc[:, 0:ATTN_HEAD_DIM] / den
        if g + 1 < n_groups:
            mx = row_max(next_parts)
    o_ref[...] = jnp.concatenate(
        [oh_ref[h // hp, (h % hp) * Q_BLOCK:(h % hp + 1) * Q_BLOCK, :] for h in range(ATTN_HEADS)],
        axis=1).astype(BF16)


def dsa_attention(zrow, zkv, idx_t, bias_tiles, *, qb0, qb1, hp=2):
    b, s, _ = zrow.shape
    sc = qb1 * Q_BLOCK
    assert sc <= 2048, "the tie-break index search covers 11 bits"
    nq = qb1 - qb0
    nqs = s // Q_BLOCK
    ng = ATTN_HEADS // hp
    rows = hp * Q_BLOCK
    topk = min(TOPK_MAX, s // TOPK_DIVISOR)
    kv_spec = lambda slab: pl.BlockSpec((None, sc, 128), lambda bi, qi: (bi, 0, slab))
    return pl.pallas_call(
        functools.partial(_dsa_kernel, qb0=qb0, topk=topk, hp=hp),
        out_shape=jax.ShapeDtypeStruct((b, nq * Q_BLOCK, ATTN_WIDTH), BF16),
        grid=(b, nq),
        in_specs=[
            pl.BlockSpec((None, Q_BLOCK, ATTN_WIDTH), lambda bi, qi: (bi, qi + qb0, COL_Q // ATTN_WIDTH)),
            kv_spec(KV_SLAB_K),
            kv_spec(KV_SLAB_V),
            kv_spec(KV_SLAB_KI),
            pl.BlockSpec((IDX_HEADS * IDX_HEAD_DIM, Q_BLOCK), lambda bi, qi: (0, bi * nqs + qi + qb0)),
            pl.BlockSpec((16, Q_BLOCK), lambda bi, qi: (IDX_HEADS * IDX_HEAD_DIM // 16, bi * nqs + qi + qb0)),
            _resident((2, ng, rows, 2 * Q_BLOCK)),
        ],
        out_specs=pl.BlockSpec((None, Q_BLOCK, ATTN_WIDTH), lambda bi, qi: (bi, qi, 0)),
        scratch_shapes=[
            pltpu.VMEM((sc, AUG_DEPTH), BF16),
            pltpu.VMEM((ng, rows, AUG_DEPTH), BF16),
            pltpu.VMEM((sc, Q_BLOCK), I32),
            pltpu.VMEM((sc, Q_BLOCK), I16),
            pltpu.VMEM((sc, Q_BLOCK), I16),
            pltpu.VMEM((sc, Q_BLOCK), I32),
            pltpu.VMEM((sc, Q_BLOCK), BF16),
            pltpu.VMEM((ng, rows, ATTN_HEAD_DIM), F32),
            pltpu.VMEM((2, rows, sc + 2 * Q_BLOCK), F32),
        ],
        compiler_params=_params("parallel", "arbitrary"),
        name="dsa_attention",
    )(zrow, zkv, zkv, zkv, idx_t, idx_t, bias_tiles.reshape(2, ng, rows, 2 * Q_BLOCK))


def _merge_kernel(x_ref, gpre_ref, wg_ref, pool_ref, wpool_ref, attn_ref, wattn_ref, s5_ref, wglu_ref,
                  wout_ref, gpost_ref, o_ref):
    x = x_ref[...]
    h = _rms(x, gpre_ref[...]).astype(BF16)
    gates = jax.nn.sigmoid(_dot(h, wg_ref[...]))
    y_pool = _dot(pool_ref[...], wpool_ref[...])
    y_attn = _dot(attn_ref[...], wattn_ref[...])
    glu = _dot(s5_ref[...], wglu_ref[...])
    y_s5 = glu[:, :D_MODEL] * jax.nn.sigmoid(glu[:, D_MODEL:])
    merged = (gates[:, 0:D_MODEL] * y_pool + gates[:, D_MODEL:2 * D_MODEL] * y_attn
              + gates[:, 2 * D_MODEL:3 * D_MODEL] * y_s5)
    mixed = _dot(merged.astype(BF16), wout_ref[...])
    o_ref[...] = x + _rms(mixed, gpost_ref[...])


def merge(x, g_pre, w_gates, pool_y, w_pool, attn_y, w_attn, s5_y, w_glu, w_out, g_post, *, tm=256):
    t, d = x.shape
    row = lambda width: pl.BlockSpec((tm, width), lambda i: (i, 0))
    full = lambda a: _resident(a.shape)
    return pl.pallas_call(
        _merge_kernel,
        out_shape=jax.ShapeDtypeStruct((t, d), F32),
        grid=(t // tm,),
        in_specs=[row(d), full(g_pre), full(w_gates), row(POOL_WIDTH), full(w_pool), row(ATTN_WIDTH),
                  full(w_attn), row(S5_WIDTH), full(w_glu), full(w_out), full(g_post)],
        out_specs=row(d),
        compiler_params=_params("parallel"),
        name="merge",
    )(x, g_pre, w_gates, pool_y, w_pool, attn_y, w_attn, s5_y, w_glu, w_out, g_post)


def _ffn_kernel(x_ref, gpre_ref, win_ref, wdown_ref, gpost_ref, o_ref, *, th):
    x = x_ref[...]
    h = _rms(x, gpre_ref[...]).astype(BF16)
    hid = wdown_ref.shape[0]
    f = None
    for c0 in range(0, hid, th):
        gate = _dot(h, win_ref[:, c0:c0 + th])
        up = _dot(h, win_ref[:, hid + c0:hid + c0 + th])
        act = (jax.nn.silu(gate) * up).astype(BF16)
        part = _dot(act, wdown_ref[c0:c0 + th, :])
        f = part if f is None else f + part
    o_ref[...] = x + _rms(f, gpost_ref[...])


def ffn(x, g_pre, w_in, w_down, g_post, *, tm=512, th=256):
    t, d = x.shape
    return pl.pallas_call(
        functools.partial(_ffn_kernel, th=th),
        out_shape=jax.ShapeDtypeStruct((t, d), F32),
        grid=(t // tm,),
        in_specs=[
            pl.BlockSpec((tm, d), lambda i: (i, 0)),
            _resident((1, d)),
            _resident(w_in.shape),
            _resident(w_down.shape),
            _resident((1, d)),
        ],
        out_specs=pl.BlockSpec((tm, d), lambda i: (i, 0)),
        compiler_params=_params("parallel"),
        name="ffn",
    )(x, g_pre, w_in, w_down, g_post)


def _pad_cols(w, width):
    return jnp.pad(w, ((0, 0), (0, width - w.shape[1])))


def _split_w_in(w):
    sizes = (POOL_WIDTH, ATTN_WIDTH, ATTN_HEAD_DIM, ATTN_HEAD_DIM, IDX_HEADS * IDX_HEAD_DIM, IDX_HEAD_DIM,
             IDX_HEADS, S5_WIDTH, N_BRANCHES * D_MODEL)
    parts, start = [], 0
    for size in sizes:
        parts.append(w[:, start:start + size])
        start += size
    w_pool, w_q, w_k, w_v, w_qi, w_ki, w_wi, w_s5, w_gates = parts
    w_rows = jnp.concatenate(
        [w_pool, w_s5, w_q, _pad_cols(w_k, 128), _pad_cols(w_v, 128), _pad_cols(w_ki, 128)], axis=1)
    w_cols = jnp.concatenate([w_qi, _pad_cols(w_wi, IDX_T_ROWS - IDX_HEADS * IDX_HEAD_DIM)], axis=1).T
    return w_rows.astype(BF16), w_cols.astype(BF16), w_gates.astype(BF16)


def _block_diag(blocks):
    g, r, c = blocks.shape
    eye = jnp.eye(g, dtype=blocks.dtype)
    return (blocks[:, :, None, :] * eye[:, None, :, None]).reshape(g * r, g * c)


def _attn_classes(nqs):
    n_classes = 8 if nqs % 8 == 0 and nqs >= 16 else 1
    step = nqs // n_classes
    return [(c * step, (c + 1) * step) for c in range(n_classes)]


def kernel(x, norm_mix_pre, norm_mix_post, norm_ffn_pre, norm_ffn_post, w_in, pool_mix_w, pool_scale, pool_out_w, attn_out_w, rel_bias, s5_lambda_re, s5_lambda_im, s5_log_dt, s5_b_re, s5_b_im, s5_c_re, s5_c_im, s5_d, s5_glu_w, w_out, ffn_w_in, ffn_w_out):
    b, s, d = x.shape
    t = b * s
    depth = w_in.shape[0]
    bias_tiles = rel_bias_tiles(rel_bias)
    xf = x.reshape(t, d)
    for l in range(depth):
        w_rows, w_cols, w_gates = _split_w_in(w_in[l])
        g_pre = norm_mix_pre[l][None, :]
        zrow, zkv, idx_t = in_proj(xf, g_pre, w_rows, w_cols)
        zrow = zrow.reshape(b, s, COL_K)
        zkv = zkv.reshape(b, s, ROW_WIDTH - COL_K)

        pool_y = pool_mixer(zrow, pool_mix_w[l].astype(BF16), pool_scale[l][None, :])

        a_re, a_im, bb_re, bb_im = s5_discretise(
            s5_lambda_re[l][:, None, :], s5_lambda_im[l][:, None, :], s5_log_dt[l][:, None, None],
            jnp.swapaxes(s5_b_re[l], 1, 2), jnp.swapaxes(s5_b_im[l], 1, 2))
        s5_y = s5_scan(
            zrow,
            _block_diag(bb_re).astype(BF16), _block_diag(bb_im).astype(BF16),
            _block_diag(jnp.swapaxes(s5_c_re[l], 1, 2)).astype(BF16),
            _block_diag(jnp.swapaxes(s5_c_im[l], 1, 2)).astype(BF16),
            a_re.reshape(1, S5_LANES), a_im.reshape(1, S5_LANES), s5_d[l][None, :])

        attn_y = jnp.concatenate(
            [dsa_attention(zrow, zkv, idx_t, bias_tiles, qb0=q0, qb1=q1)
             for q0, q1 in _attn_classes(s // Q_BLOCK)], axis=1)

        xf = merge(xf, g_pre, w_gates, pool_y.reshape(t, POOL_WIDTH), pool_out_w[l].astype(BF16),
                   attn_y.reshape(t, ATTN_WIDTH), attn_out_w[l].astype(BF16),
                   s5_y.reshape(t, S5_WIDTH), s5_glu_w[l].astype(BF16), w_out[l].astype(BF16),
                   norm_mix_post[l][None, :])
        xf = ffn(xf, norm_ffn_pre[l][None, :], ffn_w_in[l].astype(BF16), ffn_w_out[l].astype(BF16),
                 norm_ffn_post[l][None, :])
    return xf.reshape(b, s, d)
```
